```python
import math
import jax, jax.numpy as jnp
from jax import lax
import numpy as np


D_MODEL = 1024
BATCH = 8
SEQ = 2048
DEPTH = 2
DEC_BATCH = 128
DEC_SEQ = 8
PAST_LEN = 16384
PAGE_SIZE = 128

HEAD_DIM = 64
DN_HEADS = D_MODEL // 128
GLA_HEADS = D_MODEL // 256
RET_HEADS = D_MODEL // 256
DN_W = DN_HEADS * HEAD_DIM
GLA_W = GLA_HEADS * HEAD_DIM
RET_W = RET_HEADS * HEAD_DIM
MIX_W = DN_W + GLA_W + RET_W
CONV_W = 4
GLA_RANK = 16
GLA_TAU = 16.0
CHUNK = 64
ROPE_THETA = 10000.0
N_GROUPS = 4
EXP_PER_GROUP = 8
N_EXPERTS = N_GROUPS * EXP_PER_GROUP
TOP_K_IN_GROUP = 2
D_EXPERT = D_MODEL // 2
EPS = 1e-6
IN_SIZES = (3 * DN_W, DN_W, DN_HEADS, DN_HEADS,
            GLA_W, GLA_W, GLA_W, GLA_W, GLA_RANK,
            RET_W, RET_W, RET_W, RET_W)
IN_COLS = sum(IN_SIZES)

kernel_name = "hymba_style_deltanet_gla_retention_hmoe_step"

F32 = jnp.float32


def rmsnorm(x, w):
    xf = x.astype(F32)
    y = xf * lax.rsqrt(jnp.mean(xf * xf, axis=-1, keepdims=True) + EPS)
    return (y * w.astype(F32)).astype(x.dtype)


def l2norm(x):
    xf = x.astype(F32)
    return xf * lax.rsqrt(jnp.sum(xf * xf, axis=-1, keepdims=True) + EPS)


def heads(t):
    B, L, _ = t.shape
    return t.reshape(B, L, -1, HEAD_DIM).transpose(0, 2, 1, 3)


def head_norm(o, w):
    B, H, L, d = o.shape
    return rmsnorm(o.transpose(0, 2, 1, 3), w).reshape(B, L, H * d)


def to_chunks(x, c):
    B, H, L = x.shape[:3]
    return jnp.moveaxis(x.reshape(B, H, L // c, c, *x.shape[3:]), 2, 0)


def from_chunks(x):
    n, B, H, c = x.shape[:4]
    return jnp.moveaxis(x, 0, 2).reshape(B, H, n * c, *x.shape[4:])


def rotary(x, pos):
    half = HEAD_DIM // 2
    inv = ROPE_THETA ** (-jnp.arange(half, dtype=F32) / half)
    ang = pos.astype(F32)[:, None] * inv[None, :]
    cos, sin = jnp.cos(ang), jnp.sin(ang)
    xf = x.astype(F32)
    x1, x2 = xf[..., :half], xf[..., half:]
    return jnp.concatenate([x1 * cos - x2 * sin, x1 * sin + x2 * cos], axis=-1)


def gated_delta_rule(q, k, v, g, beta, s0):
    L = q.shape[2]
    c = math.gcd(CHUNK, L)
    qc, kc, vc = (to_chunks(t.astype(F32), c) for t in (q, k, v))
    gc = to_chunks(g.astype(F32), c)
    bc = to_chunks(beta.astype(F32), c)
    tril = jnp.tril(jnp.ones((c, c), bool))
    strict = jnp.tril(jnp.ones((c, c), bool), -1)
    eye = jnp.eye(c, dtype=F32)

    def step(S, inp):
        qi, ki, vi, gi, bi = inp
        G = jnp.cumsum(gi, axis=-1)
        diff = G[..., :, None] - G[..., None, :]
        decay = jnp.where(tril, jnp.exp(jnp.where(tril, diff, 0.0)), 0.0)
        kk = jnp.einsum('bhtd,bhsd->bhts', ki, ki)
        A = jnp.where(strict, bi[..., :, None] * kk * decay, 0.0)
        T = lax.linalg.triangular_solve(A, jnp.broadcast_to(eye, A.shape), left_side=True,
                                        lower=True, unit_diagonal=True)
        eG = jnp.exp(G)
        U = T @ (vi * bi[..., None]) - (T @ (ki * (bi * eG)[..., None])) @ S
        qk = jnp.einsum('bhtd,bhsd->bhts', qi, ki) * decay
        o = jnp.einsum('bhtd,bhdv->bhtv', qi * eG[..., None], S) + jnp.einsum('bhts,bhsv->bhtv', qk, U)
        Gl = G[..., -1:]
        S = S * jnp.exp(Gl)[..., None] + jnp.einsum('bhsd,bhsv->bhdv', ki * jnp.exp(Gl - G)[..., None], U)
        return S, o

    S, o = lax.scan(step, s0.astype(F32), (qc, kc, vc, gc, bc))
    return from_chunks(o), S


def gla_chunked(q, k, v, lg, s0):
    L = q.shape[2]
    c = math.gcd(CHUNK, L)
    qc, kc, vc, gc = (to_chunks(t.astype(F32), c) for t in (q, k, v, lg))
    tril = jnp.tril(jnp.ones((c, c), bool))[..., None]

    def step(S, inp):
        qi, ki, vi, gi = inp
        b = jnp.cumsum(gi, axis=-2)
        diff = b[..., :, None, :] - b[..., None, :, :]
        w = jnp.where(tril, jnp.exp(jnp.where(tril, diff, 0.0)), 0.0)
        A = jnp.einsum('bhtd,bhsd,bhtsd->bhts', qi, ki, w)
        o = jnp.einsum('bhtd,bhdv->bhtv', qi * jnp.exp(b), S) + jnp.einsum('bhts,bhsv->bhtv', A, vi)
        bl = b[..., -1:, :]
        S = S * jnp.exp(bl[..., 0, :])[..., None] + jnp.einsum('bhsd,bhsv->bhdv', ki * jnp.exp(bl - b), vi)
        return S, o

    S, o = lax.scan(step, s0.astype(F32), (qc, kc, vc, gc))
    return from_chunks(o), S


def retention_chunked(q, k, v, s0):
    B, H, L, _ = q.shape
    c = math.gcd(CHUNK, L)
    lg = jnp.log(1.0 - 2.0 ** (-5.0 - jnp.arange(H, dtype=F32)))
    idx = jnp.arange(c, dtype=F32)
    rel = idx[:, None] - idx[None, :]
    dmat = jnp.where(rel >= 0, jnp.exp(jnp.maximum(rel, 0.0)[None] * lg[:, None, None]), 0.0)
    q_in = jnp.exp((idx + 1.0)[None, :] * lg[:, None])[None, :, :, None]
    k_out = jnp.exp((c - 1.0 - idx)[None, :] * lg[:, None])[None, :, :, None]
    s_dec = jnp.exp(c * lg)[None, :, None, None]
    qc, kc, vc = (to_chunks(t.astype(F32), c) for t in (q, k, v))

    def step(S, inp):
        qi, ki, vi = inp
        att = jnp.einsum('bhtd,bhsd->bhts', qi, ki) * dmat
        o = jnp.einsum('bhtd,bhdv->bhtv', qi * q_in, S) + jnp.einsum('bhts,bhsv->bhtv', att, vi)
        S = S * s_dec + jnp.einsum('bhsd,bhsv->bhdv', ki * k_out, vi)
        return S, o

    S, o = lax.scan(step, s0.astype(F32), (qc, kc, vc))
    return from_chunks(o), S


def token_mixers(h, pos, conv_buf, dn_s, gla_s, ret_s, w_in, conv_w, a_log, dt_bias, dn_norm,
                 gla_w2, gla_b2, gla_norm, ret_norm, w_out):
    B, L, _ = h.shape
    proj = h @ w_in
    offs = np.cumsum(IN_SIZES)[:-1].tolist()
    (qkv, dn_z, dn_a, dn_b, gq, gk, gv, gr, gu, rq, rk, rv, rg) = jnp.split(proj, offs, axis=-1)

    xpad = jnp.concatenate([conv_buf.astype(qkv.dtype), qkv], axis=1)
    conv = sum(xpad[:, j:j + L] * conv_w[j] for j in range(CONV_W))
    new_buf = xpad[:, L:]
    dq, dk, dv = jnp.split(jax.nn.silu(conv), 3, axis=-1)
    dq = l2norm(heads(dq)) * (HEAD_DIM ** -0.5)
    dk = l2norm(heads(dk))
    g = -jnp.exp(a_log.astype(F32)) * jax.nn.softplus(dn_a.astype(F32) + dt_bias.astype(F32))
    beta = jax.nn.sigmoid(dn_b.astype(F32))
    o, dn_new = gated_delta_rule(dq, dk, heads(dv), g.transpose(0, 2, 1), beta.transpose(0, 2, 1), dn_s)
    o_dn = head_norm(o, dn_norm) * jax.nn.silu(dn_z.astype(F32))

    lg = jax.nn.log_sigmoid((gu @ gla_w2 + gla_b2).astype(F32)) / GLA_TAU
    o, gla_new = gla_chunked(heads(gq).astype(F32) * (HEAD_DIM ** -0.5), heads(gk), heads(gv), heads(lg), gla_s)
    o_gla = head_norm(o, gla_norm) * jax.nn.silu(gr.astype(F32))

    q_r = rotary(heads(rq), pos)
    k_r = rotary(heads(rk), pos) * (HEAD_DIM ** -0.5)
    o, ret_new = retention_chunked(q_r, k_r, heads(rv), ret_s)
    o_ret = head_norm(o, ret_norm) * jax.nn.silu(rg.astype(F32))

    mixed = jnp.concatenate([o_dn, o_gla, o_ret], axis=-1).astype(h.dtype)
    return (mixed @ w_out).astype(h.dtype), new_buf, dn_new, gla_new, ret_new


def hier_moe(h, w_group, b_group, w_router, b_router, w_gate, w_up, w_down):
    B, L, D = h.shape
    T = B * L
    x = h.reshape(T, D)
    gl = (x @ w_group).astype(F32) + b_group.astype(F32)
    gsel = jnp.argmax(gl, axis=-1)
    gprob = jnp.take_along_axis(jax.nn.softmax(gl, axis=-1), gsel[:, None], axis=-1)
    el = ((x @ w_router).astype(F32) + b_router.astype(F32)).reshape(T, N_GROUPS, EXP_PER_GROUP)
    el_sel = jnp.take_along_axis(el, gsel[:, None, None], axis=1)[:, 0]
    topv, topi = lax.top_k(jax.nn.softmax(el_sel, axis=-1), TOP_K_IN_GROUP)
    weights = gprob * topv / jnp.sum(topv, axis=-1, keepdims=True)
    expert_id = gsel[:, None] * EXP_PER_GROUP + topi
    combine = jnp.sum(jax.nn.one_hot(expert_id, N_EXPERTS, dtype=F32) * weights[..., None], axis=1)
    out = jnp.zeros((T, D), F32)
    for e in range(N_EXPERTS):
        hid = jax.nn.silu(x @ w_gate[e]) * (x @ w_up[e])
        out = out + combine[:, e:e + 1] * (hid @ w_down[e]).astype(F32)
    return out.reshape(B, L, D).astype(h.dtype)


def trunk(x, pos, conv0, dn0, gla0, ret0, norm_mix, w_in, dn_conv_w, dn_a_log, dn_dt_bias, dn_norm,
          gla_w2, gla_b2, gla_norm, ret_norm, w_out, norm_ffn, w_group, b_group, w_router, b_router,
          w_gate, w_up, w_down, norm_final):
    h = x
    convs, dns, glas, rets = [], [], [], []
    for l in range(DEPTH):
        a, cb, s_dn, s_gla, s_ret = token_mixers(
            rmsnorm(h, norm_mix[l]), pos, conv0[l], dn0[l], gla0[l], ret0[l], w_in[l], dn_conv_w[l],
            dn_a_log[l], dn_dt_bias[l], dn_norm[l], gla_w2[l], gla_b2[l], gla_norm[l], ret_norm[l], w_out[l])
        h = h + a
        h = h + hier_moe(rmsnorm(h, norm_ffn[l]), w_group[l], b_group[l], w_router[l], b_router[l],
                         w_gate[l], w_up[l], w_down[l])
        convs.append(cb); dns.append(s_dn); glas.append(s_gla); rets.append(s_ret)
    return (rmsnorm(h, norm_final), jnp.stack(convs), jnp.stack(dns), jnp.stack(glas), jnp.stack(rets))


def setup_inputs(seed: int = 0) -> dict:
    key = jax.random.key(seed)
    ks = jax.random.split(key, 32)

    def nrm(k, shape, s):
        return jax.random.normal(k, shape, F32) * s

    dt = jnp.exp(jax.random.uniform(ks[10], (DEPTH, DN_HEADS), F32, math.log(1e-3), math.log(1e-1)))
    return {
        "x_prompt": nrm(ks[0], (BATCH, SEQ, D_MODEL), 1.0),
        "x_sample": nrm(ks[1], (DEC_BATCH, DEC_SEQ, D_MODEL), 1.0),
        "state_dn_conv": nrm(ks[2], (DEPTH, DEC_BATCH, CONV_W - 1, 3 * DN_W), 1.0),
        "state_dn": nrm(ks[3], (DEPTH, DEC_BATCH, DN_HEADS, HEAD_DIM, HEAD_DIM), 0.5),
        "state_gla": nrm(ks[4], (DEPTH, DEC_BATCH, GLA_HEADS, HEAD_DIM, HEAD_DIM), 0.5),
        "state_ret": nrm(ks[5], (DEPTH, DEC_BATCH, RET_HEADS, HEAD_DIM, HEAD_DIM), 0.5),
        "norm_mix": 1.0 + nrm(ks[6], (DEPTH, D_MODEL), 0.02),
        "w_in": nrm(ks[7], (DEPTH, D_MODEL, IN_COLS), D_MODEL ** -0.5),
        "dn_conv_w": nrm(ks[8], (DEPTH, CONV_W, 3 * DN_W), CONV_W ** -0.5),
        "dn_a_log": jnp.log(jax.random.uniform(ks[9], (DEPTH, DN_HEADS), F32, 1.0, 16.0)),
        "dn_dt_bias": dt + jnp.log(-jnp.expm1(-dt)),
        "dn_norm": 1.0 + nrm(ks[11], (DEPTH, HEAD_DIM), 0.02),
        "gla_w2": nrm(ks[12], (DEPTH, GLA_RANK, GLA_W), GLA_RANK ** -0.5),
        "gla_b2": nrm(ks[13], (DEPTH, GLA_W), 0.1),
        "gla_norm": 1.0 + nrm(ks[14], (DEPTH, HEAD_DIM), 0.02),
        "ret_norm": 1.0 + nrm(ks[15], (DEPTH, HEAD_DIM), 0.02),
        "w_out": nrm(ks[16], (DEPTH, MIX_W, D_MODEL), MIX_W ** -0.5),
        "norm_ffn": 1.0 + nrm(ks[17], (DEPTH, D_MODEL), 0.02),
        "w_group": nrm(ks[18], (DEPTH, D_MODEL, N_GROUPS), D_MODEL ** -0.5),
        "b_group": nrm(ks[19], (DEPTH, N_GROUPS), 0.01),
        "w_router": nrm(ks[20], (DEPTH, D_MODEL, N_EXPERTS), D_MODEL ** -0.5),
        "b_router": nrm(ks[21], (DEPTH, N_EXPERTS), 0.01),
        "w_gate": nrm(ks[22], (DEPTH, N_EXPERTS, D_MODEL, D_EXPERT), D_MODEL ** -0.5),
        "w_up": nrm(ks[23], (DEPTH, N_EXPERTS, D_MODEL, D_EXPERT), D_MODEL ** -0.5),
        "w_down": nrm(ks[24], (DEPTH, N_EXPERTS, D_EXPERT, D_MODEL), D_EXPERT ** -0.5),
        "norm_final": 1.0 + nrm(ks[25], (D_MODEL,), 0.02),
    }


def reference(x_prompt, x_sample, state_dn_conv, state_dn, state_gla, state_ret, norm_mix, w_in,
              dn_conv_w, dn_a_log, dn_dt_bias, dn_norm, gla_w2, gla_b2, gla_norm, ret_norm, w_out,
              norm_ffn, w_group, b_group, w_router, b_router, w_gate, w_up, w_down, norm_final):
    B, L = x_prompt.shape[0], x_prompt.shape[1]
    zc = jnp.zeros((DEPTH, B, CONV_W - 1, 3 * DN_W), x_prompt.dtype)
    z_dn = jnp.zeros((DEPTH, B, DN_HEADS, HEAD_DIM, HEAD_DIM), F32)
    z_gla = jnp.zeros((DEPTH, B, GLA_HEADS, HEAD_DIM, HEAD_DIM), F32)
    z_ret = jnp.zeros((DEPTH, B, RET_HEADS, HEAD_DIM, HEAD_DIM), F32)
    y_prompt, p_conv, p_dn, p_gla, p_ret = trunk(
        x_prompt, jnp.arange(L), zc, z_dn, z_gla, z_ret, norm_mix, w_in, dn_conv_w, dn_a_log, dn_dt_bias,
        dn_norm, gla_w2, gla_b2, gla_norm, ret_norm, w_out, norm_ffn, w_group, b_group, w_router, b_router,
        w_gate, w_up, w_down, norm_final)
    pos_s = PAST_LEN + jnp.arange(x_sample.shape[1])
    y_sample, s_conv, s_dn, s_gla, s_ret = trunk(
        x_sample, pos_s, state_dn_conv, state_dn, state_gla, state_ret, norm_mix, w_in, dn_conv_w, dn_a_log,
        dn_dt_bias, dn_norm, gla_w2, gla_b2, gla_norm, ret_norm, w_out, norm_ffn, w_group, b_group, w_router,
        b_router, w_gate, w_up, w_down, norm_final)
    return (y_prompt, y_sample, p_conv, p_dn, p_gla, p_ret, s_conv, s_dn, s_gla, s_ret)
```

```python
import functools
import math

import numpy as np
import jax
import jax.numpy as jnp
from jax import lax
from jax.experimental import pallas as pl
from jax.experimental.pallas import tpu as pltpu

F32 = jnp.float32
BF16 = jnp.bfloat16
I32 = jnp.int32

D_MODEL = 1024
HEAD_DIM = 64
DN_HEADS = 8
GLA_HEADS = 4
RET_HEADS = 4
DN_W = DN_HEADS * HEAD_DIM
GLA_W = GLA_HEADS * HEAD_DIM
RET_W = RET_HEADS * HEAD_DIM
CONV_W = 4
GLA_RANK = 16
GLA_TAU = 16.0
ROPE_THETA = 10000.0
N_GROUPS = 4
EXP_PER_GROUP = 8
N_EXPERTS = N_GROUPS * EXP_PER_GROUP
D_EXPERT = D_MODEL // 2
EPS = 1e-6
PAST_LEN = 16384

LANES = 128
ROWS = 64
PROJ_BIG = 4096
PROJ_W = PROJ_BIG + LANES
C_QKV, C_DNZ = 0, 1536
C_GQ, C_GK, C_GV, C_GR = 2048, 2304, 2560, 2816
C_RQ, C_RK, C_RV, C_RG = 3072, 3328, 3584, 3840
C_SMALL = 4096
ROUTE_LANE0 = N_GROUPS
VMEM_LIMIT = 48 * 1024 * 1024


def _mm(a, b):
    return jnp.dot(a, b, preferred_element_type=F32)


def _mm_nt(a, b):
    return lax.dot_general(a, b, (((1,), (1,)), ((), ())), preferred_element_type=F32)


def _mm_tn(a, b):
    return lax.dot_general(a, b, (((0,), (0,)), ((), ())), preferred_element_type=F32)


def _split(x, n):
    parts = []
    r = x
    for i in range(n):
        p = r.astype(BF16)
        parts.append(p)
        if i + 1 < n:
            r = r - p.astype(F32)
    return parts


def _xmm(x, cst, n=3):
    acc = None
    for p in _split(x, n):
        t = _mm(p, cst)
        acc = t if acc is None else acc + t
    return acc


def _cmm(cst, x, n=3):
    acc = None
    for p in _split(x, n):
        t = _mm(cst, p)
        acc = t if acc is None else acc + t
    return acc


def _xmm_nt(cst, x, n=3):
    acc = None
    for p in _split(x, n):
        t = _mm_nt(cst, p)
        acc = t if acc is None else acc + t
    return acc


def _mm2(a, b):
    ah, al = _split(a, 2)
    bh, bl = _split(b, 2)
    return _mm(ah, bh) + (_mm(ah, bl) + _mm(al, bh))


def _sigmoid(x):
    return 1.0 / (1.0 + jnp.exp(-x))


def _silu(x):
    return x * _sigmoid(x)


def _softplus(x):
    return jnp.maximum(x, 0.0) + jnp.log1p(jnp.exp(-jnp.abs(x)))


def _rmsnorm_rows(x, w):
    ms = jnp.mean(x * x, axis=-1, keepdims=True)
    return x * lax.rsqrt(ms + EPS) * w


def _in_proj_body(h_ref, nw_ref, w_ref, o_ref):
    xn = _rmsnorm_rows(h_ref[...], nw_ref[...])
    o_ref[...] = _mm(xn.astype(BF16), w_ref[...])


def _in_proj(h, norm_w, w_bf, tm):
    T = h.shape[0]
    return pl.pallas_call(
        _in_proj_body,
        grid=(T // tm,),
        in_specs=[
            pl.BlockSpec((tm, D_MODEL), lambda i: (i, 0)),
            pl.BlockSpec((1, D_MODEL), lambda i: (0, 0)),
            pl.BlockSpec((D_MODEL, PROJ_W), lambda i: (0, 0)),
        ],
        out_specs=pl.BlockSpec((tm, PROJ_W), lambda i: (i, 0)),
        out_shape=jax.ShapeDtypeStruct((T, PROJ_W), F32),
        compiler_params=pltpu.CompilerParams(
            dimension_semantics=("arbitrary",), vmem_limit_bytes=VMEM_LIMIT),
        name="in_proj",
    )(h, norm_w.reshape(1, D_MODEL), w_bf)


def _mixer_constants(nseq, cseq, n_blk):
    r = np.arange(ROWS)
    seq = r // cseq
    tl = r % cseq
    same = seq[:, None] == seq[None, :]
    incl = same & (tl[None, :] <= tl[:, None])
    strict = same & (tl[None, :] < tl[:, None])
    two = lambda m: np.concatenate([m, m], axis=1)
    eye = np.eye(ROWS)
    sel = np.zeros((ROWS, LANES))
    sel[:, 0] = 1.0
    sel[:, HEAD_DIM] = 1.0
    lane_head = np.arange(LANES) // HEAD_DIM
    bd = (lane_head[:, None] == lane_head[None, :]).astype(np.float32)
    eg = np.zeros((LANES, DN_W))
    eb = np.zeros((LANES, DN_W))
    for hh in range(DN_HEADS):
        eg[hh, hh * HEAD_DIM:(hh + 1) * HEAD_DIM] = 1.0
        eb[DN_HEADS + hh, hh * HEAD_DIM:(hh + 1) * HEAD_DIM] = 1.0
    rb = ROWS // n_blk
    blk = np.stack([np.broadcast_to((r < rb * (i + 1))[:, None], (ROWS, LANES)) for i in range(n_blk)])
    ii = np.concatenate([np.eye(HEAD_DIM), np.eye(HEAD_DIM)], axis=1)
    consts = dict(
        lcum=jnp.asarray(incl, BF16), lall=jnp.asarray(same, BF16),
        mincl=jnp.asarray(two(incl), F32), mstr=jnp.asarray(two(strict), F32),
        i2=jnp.asarray(two(eye), F32), sel=jnp.asarray(sel, BF16),
        bd=jnp.asarray(bd, F32), bsum=jnp.asarray(bd, BF16),
        eg=jnp.asarray(eg, BF16), eb=jnp.asarray(eb, BF16),
        blk=jnp.asarray(blk, F32),
        ii=jnp.asarray(ii, BF16), iit=jnp.asarray(ii.T, BF16),
    )
    lg = jnp.log(1.0 - 2.0 ** (-5.0 - jnp.arange(RET_HEADS, dtype=F32)))
    tlf = jnp.asarray(tl, F32)
    rel = tlf[:, None] - tlf[None, :]
    dmat = jnp.where(jnp.asarray(incl)[None], jnp.exp(jnp.maximum(rel, 0.0)[None] * lg[:, None, None]), 0.0)
    qin = jnp.exp((tlf + 1.0)[None, :] * lg[:, None])
    kout = jnp.exp((cseq - 1.0 - tlf)[None, :] * lg[:, None])
    sdec = jnp.exp(cseq * lg)
    pair = lambda a, b: jnp.concatenate([a, b], axis=-1)
    lanes = lambda v: jnp.broadcast_to(v[:, None], (ROWS, HEAD_DIM))
    consts["dmat"] = jnp.stack([pair(dmat[2 * p], dmat[2 * p + 1]) for p in range(RET_HEADS // 2)])
    consts["qin"] = jnp.stack([pair(lanes(qin[2 * p]), lanes(qin[2 * p + 1])) for p in range(RET_HEADS // 2)])
    consts["kout"] = jnp.stack([pair(lanes(kout[2 * p]), lanes(kout[2 * p + 1])) for p in range(RET_HEADS // 2)])
    consts["sdec"] = jnp.stack([pair(jnp.full((1, HEAD_DIM), sdec[2 * p]), jnp.full((1, HEAD_DIM), sdec[2 * p + 1]))
                                for p in range(RET_HEADS // 2)])
    return consts


_CONST_ORDER = ("lcum", "lall", "mincl", "mstr", "i2", "sel", "bd", "bsum", "eg", "eb", "blk", "ii", "iit",
                "dmat", "qin", "kout", "sdec")


def _mixer_body(nseq, cseq, n_blk, n_sq, G, *refs):
    proj_refs = refs[:G]
    refs = refs[G:]
    (conv_in_ref, dn_in_ref, gla_in_ref, ret_in_ref, rc_ref, rs_ref,
     convw_ref, alog_ref, dtb_ref, w2_ref, b2_ref, nrm_ref,
     lcum_ref, lall_ref, mincl_ref, mstr_ref, i2_ref, sel_ref, bd_ref, bsum_ref, eg_ref, eb_ref, blk_ref,
     ii_ref, iit_ref, dmat_ref, qin_ref, kout_ref, sdec_ref,
     mixed_ref, conv_out_ref, dn_out_ref, gla_out_ref, ret_out_ref,
     ext_ref, sdn_ref, sgla_ref, sret_ref) = refs
    ci = pl.program_id(1)
    last = pl.num_programs(1) - 1

    lane = lax.broadcasted_iota(I32, (1, LANES), 1)
    mlo = (lane < HEAD_DIM).astype(F32)
    mhi = 1.0 - mlo
    half_lo = (lane % HEAD_DIM) < (HEAD_DIM // 2)
    rid = lax.broadcasted_iota(I32, (ROWS, LANES), 0)
    bd = bd_ref[...]
    lcum = lcum_ref[...]
    lall = lall_ref[...]
    mincl = mincl_ref[...]
    mstr = mstr_ref[...]
    bsum = bsum_ref[...]

    def stack(x):
        return jnp.concatenate([x * mlo, x * mhi], axis=0).astype(BF16)

    def seq_rows(x, j):
        if nseq == 1:
            return x
        return jnp.where((rid >= j * cseq) & (rid < (j + 1) * cseq), x, 0.0)

    def seq_row1(x, j):
        return x[j * cseq:j * cseq + 1, :]

    def state_mm(x, s_ref, g, p, nt=False):
        acc = None
        for j in range(nseq):
            s = s_ref[g, j, p].astype(BF16)
            xj = seq_rows(x, j).astype(BF16)
            t = _mm_nt(xj, s) if nt else _mm(xj, s)
            acc = t if acc is None else acc + t
        return acc

    @pl.when(ci == 0)
    def _():
        for g in range(G):
            ext_ref[g, :, 8 - (CONV_W - 1):8, :] = conv_in_ref[g]
            for j in range(nseq):
                for p in range(DN_HEADS // 2):
                    sdn_ref[g, j, p] = _xmm(dn_in_ref[g, j, p], ii_ref[...]) * bd
                for p in range(GLA_HEADS // 2):
                    sgla_ref[g, j, p] = _xmm(gla_in_ref[g, j, p], ii_ref[...]) * bd
                for p in range(RET_HEADS // 2):
                    sret_ref[g, j, p] = _xmm(ret_in_ref[g, j, p], ii_ref[...]) * bd

    for g in range(G):
        qkv = proj_refs[g][0, :, C_QKV:C_QKV + 3 * DN_W]
        ext_ref[g, :, 8:8 + cseq, :] = qkv.reshape(nseq, cseq, 3 * DN_W)
        conv = None
        for jj in range(CONV_W):
            st = 8 - (CONV_W - 1) + jj
            term = ext_ref[g, :, st:st + cseq, :] * convw_ref[jj:jj + 1, :]
            conv = term if conv is None else conv + term
        new_buf = ext_ref[g, :, 8 + cseq - (CONV_W - 1):8 + cseq, :]
        ext_ref[g, :, 8 - (CONV_W - 1):8, :] = new_buf

        @pl.when(ci == last)
        def _():
            conv_out_ref[g] = new_buf

        conv = _silu(conv.reshape(ROWS, 3 * DN_W))

        sm = proj_refs[g][0, :, C_SMALL:C_SMALL + LANES]
        gdec = -jnp.exp(alog_ref[...]) * _softplus(sm + dtb_ref[...])
        beta = _sigmoid(sm)
        gcum = _cmm(lcum, gdec)
        gtot = _cmm(lall, gdec)
        gcol_all = _xmm(gcum, eg_ref[...])
        glast_all = _xmm(gtot, eg_ref[...])
        bcol_all = _xmm(beta, eb_ref[...])

        for p in range(DN_HEADS // 2):
            ls = slice(p * LANES, (p + 1) * LANES)
            q2 = conv[:, ls]
            k2 = conv[:, DN_W + p * LANES:DN_W + (p + 1) * LANES]
            v2 = conv[:, 2 * DN_W + p * LANES:2 * DN_W + (p + 1) * LANES]
            q2 = q2 * lax.rsqrt(_xmm(q2 * q2, bsum, 2) + EPS) * (HEAD_DIM ** -0.5)
            k2 = k2 * lax.rsqrt(_xmm(k2 * k2, bsum, 2) + EPS)
            gcol = gcol_all[:, ls]
            glast = glast_all[:, ls]
            bcol = bcol_all[:, ls]
            egc = jnp.exp(gcol)
            ystack = jnp.concatenate([gcol * mlo, gcol * mhi], axis=0)
            grow = _xmm_nt(sel_ref[...], ystack)
            decay = jnp.exp(jnp.where(mincl > 0.0, gcol - grow, 0.0)) * mincl
            kst = stack(k2)
            k2b = k2.astype(BF16)
            a2 = bcol * _mm_nt(k2b, kst) * decay * mstr
            nk = -a2
            tt = i2_ref[...] + nk
            for _ in range(n_sq):
                nk = _mm2(nk, jnp.concatenate([nk * mlo, nk * mhi], axis=0))
                tt = tt + _mm2(tt, jnp.concatenate([nk * mlo, nk * mhi], axis=0))
            ttb = tt.astype(BF16)
            wv = _mm(ttb, stack(v2 * bcol))
            wk = _mm(ttb, stack(k2 * (bcol * egc)))
            u2 = wv - state_mm(wk, sdn_ref, g, p)
            qk = _mm_nt(q2.astype(BF16), kst) * decay
            o2 = state_mm(q2 * egc, sdn_ref, g, p) + _mm(qk.astype(BF16), stack(u2))
            kdec = k2 * jnp.exp(glast - gcol)
            eglast = jnp.exp(glast)
            for j in range(nseq):
                upd = _mm_tn(seq_rows(kdec, j).astype(BF16), u2.astype(BF16)) * bd
                sdn_ref[g, j, p] = sdn_ref[g, j, p] * seq_row1(eglast, j) + upd
            ms = _xmm(o2 * o2, bsum, 2) * (1.0 / HEAD_DIM)
            on = o2 * lax.rsqrt(ms + EPS) * nrm_ref[0:1, :]
            z2 = proj_refs[g][0, :, C_DNZ + p * LANES:C_DNZ + (p + 1) * LANES]
            mixed_ref[g, :, ls] = (on * _silu(z2)).astype(mixed_ref.dtype)

        xg = _mm(sm.astype(BF16), w2_ref[...]) + b2_ref[...]
        lg_all = (jnp.minimum(xg, 0.0) - jnp.log1p(jnp.exp(-jnp.abs(xg)))) * (1.0 / GLA_TAU)
        bcum_all = _cmm(lcum, lg_all)
        btot_all = _cmm(lall, lg_all)
        rb = ROWS // n_blk
        for p in range(GLA_HEADS // 2):
            ls = slice(p * LANES, (p + 1) * LANES)
            q2 = proj_refs[g][0, :, C_GQ + p * LANES:C_GQ + (p + 1) * LANES] * (HEAD_DIM ** -0.5)
            k2 = proj_refs[g][0, :, C_GK + p * LANES:C_GK + (p + 1) * LANES]
            v2 = proj_refs[g][0, :, C_GV + p * LANES:C_GV + (p + 1) * LANES]
            bc = bcum_all[:, ls]
            bt = btot_all[:, ls]
            blocks = []
            for i in range(n_blk):
                rs = slice(i * rb, (i + 1) * rb)
                if i == 0:
                    qe = q2[rs] * jnp.exp(bc[rs])
                    ke = k2 * jnp.exp(blk_ref[i] * (-bc))
                else:
                    m = bc[i * rb - 1:i * rb, :]
                    qe = q2[rs] * jnp.exp(bc[rs] - m)
                    ke = k2 * jnp.exp(blk_ref[i] * (m - bc))
                blocks.append(_mm_nt(qe.astype(BF16), stack(ke)))
            aa = (blocks[0] if n_blk == 1 else jnp.concatenate(blocks, axis=0)) * mincl
            o2 = state_mm(q2 * jnp.exp(bc), sgla_ref, g, p, nt=True) + _mm(aa.astype(BF16), stack(v2))
            kdec = k2 * jnp.exp(bt - bc)
            ebt = jnp.exp(bt)
            for j in range(nseq):
                upd = _mm_tn(seq_rows(v2, j).astype(BF16), kdec.astype(BF16)) * bd
                sgla_ref[g, j, p] = sgla_ref[g, j, p] * seq_row1(ebt, j) + upd
            ms = _xmm(o2 * o2, bsum, 2) * (1.0 / HEAD_DIM)
            on = o2 * lax.rsqrt(ms + EPS) * nrm_ref[1:2, :]
            z2 = proj_refs[g][0, :, C_GR + p * LANES:C_GR + (p + 1) * LANES]
            mixed_ref[g, :, DN_W + p * LANES:DN_W + (p + 1) * LANES] = (on * _silu(z2)).astype(mixed_ref.dtype)

        rc = rc_ref[...]
        rsn = rs_ref[...]

        def rot(x):
            swapped = jnp.where(half_lo, pltpu.roll(x, LANES - HEAD_DIM // 2, 1), pltpu.roll(x, HEAD_DIM // 2, 1))
            return x * rc + swapped * rsn

        for p in range(RET_HEADS // 2):
            q2 = rot(proj_refs[g][0, :, C_RQ + p * LANES:C_RQ + (p + 1) * LANES])
            k2 = rot(proj_refs[g][0, :, C_RK + p * LANES:C_RK + (p + 1) * LANES]) * (HEAD_DIM ** -0.5)
            v2 = proj_refs[g][0, :, C_RV + p * LANES:C_RV + (p + 1) * LANES]
            att = _mm_nt(q2.astype(BF16), stack(k2)) * dmat_ref[p]
            o2 = state_mm(q2 * qin_ref[p], sret_ref, g, p) + _mm(att.astype(BF16), stack(v2))
            kdec = k2 * kout_ref[p]
            for j in range(nseq):
                upd = _mm_tn(seq_rows(kdec, j).astype(BF16), v2.astype(BF16)) * bd
                sret_ref[g, j, p] = sret_ref[g, j, p] * sdec_ref[p] + upd
            ms = _xmm(o2 * o2, bsum, 2) * (1.0 / HEAD_DIM)
            on = o2 * lax.rsqrt(ms + EPS) * nrm_ref[2:3, :]
            z2 = proj_refs[g][0, :, C_RG + p * LANES:C_RG + (p + 1) * LANES]
            off = DN_W + GLA_W + p * LANES
            mixed_ref[g, :, off:off + LANES] = (on * _silu(z2)).astype(mixed_ref.dtype)

    @pl.when(ci == last)
    def _():
        for g in range(G):
            for j in range(nseq):
                for p in range(DN_HEADS // 2):
                    dn_out_ref[g, j, p] = _xmm(sdn_ref[g, j, p], iit_ref[...])
                for p in range(GLA_HEADS // 2):
                    gla_out_ref[g, j, p] = _xmm(sgla_ref[g, j, p], iit_ref[...])
                for p in range(RET_HEADS // 2):
                    ret_out_ref[g, j, p] = _xmm(sret_ref[g, j, p], iit_ref[...])


def _mixers(proj3, row0, NB, Lr, conv_in, dn_in, gla_in, ret_in, rot_c, rot_s, params, nseq, cseq, G):
    n_blk = 4 if cseq == ROWS else 1
    n_sq = int(math.log2(cseq)) - 1
    consts = _mixer_constants(nseq, cseq, n_blk)
    conv_w, alog, dtb, w2, b2, nrm = params
    nchunk = Lr // ROWS

    def full(a):
        nd = a.ndim
        return pl.BlockSpec(a.shape, lambda b, c, _nd=nd: (0,) * _nd)

    def per_group(a):
        nd = a.ndim
        return pl.BlockSpec((G,) + a.shape[1:], lambda b, c, _nd=nd: (b,) + (0,) * (_nd - 1))

    small = [conv_w, alog, dtb, w2, b2, nrm] + [consts[k] for k in _CONST_ORDER]
    in_specs = [
        pl.BlockSpec((1, ROWS, PROJ_W), lambda b, c, _g=g: (row0 + (b * G + _g) * nchunk + c, 0, 0))
        for g in range(G)
    ] + [
        per_group(conv_in), per_group(dn_in), per_group(gla_in), per_group(ret_in),
        pl.BlockSpec((ROWS, LANES), lambda b, c: (c, 0)),
        pl.BlockSpec((ROWS, LANES), lambda b, c: (c, 0)),
    ] + [full(a) for a in small]
    out_shape = (
        jax.ShapeDtypeStruct((NB, Lr, D_MODEL), BF16),
        jax.ShapeDtypeStruct(conv_in.shape, F32),
        jax.ShapeDtypeStruct(dn_in.shape, F32),
        jax.ShapeDtypeStruct(gla_in.shape, F32),
        jax.ShapeDtypeStruct(ret_in.shape, F32),
    )
    out_specs = (
        pl.BlockSpec((G, ROWS, D_MODEL), lambda b, c: (b, c, 0)),
        per_group(conv_in), per_group(dn_in), per_group(gla_in), per_group(ret_in),
    )
    scratch = [
        pltpu.VMEM((G, nseq, 8 + cseq, 3 * DN_W), F32),
        pltpu.VMEM((G, nseq, DN_HEADS // 2, LANES, LANES), F32),
        pltpu.VMEM((G, nseq, GLA_HEADS // 2, LANES, LANES), F32),
        pltpu.VMEM((G, nseq, RET_HEADS // 2, LANES, LANES), F32),
    ]
    return pl.pallas_call(
        functools.partial(_mixer_body, nseq, cseq, n_blk, n_sq, G),
        grid=(NB // G, nchunk),
        in_specs=in_specs,
        out_specs=out_specs,
        out_shape=out_shape,
        scratch_shapes=scratch,
        compiler_params=pltpu.CompilerParams(
            dimension_semantics=("arbitrary", "arbitrary"), vmem_limit_bytes=VMEM_LIMIT),
        name="mixers_c%d" % cseq,
    )(*([proj3] * G), conv_in, dn_in, gla_in, ret_in, rot_c, rot_s, *small)


def _out_route_body(tm, h_ref, mix_ref, wout_ref, nw_ref, wr_ref, br_ref, ltri_ref,
                    h1_ref, xp_ref, route_ref, cnt_ref, carry_ref):
    i = pl.program_id(0)

    @pl.when(i == 0)
    def _():
        carry_ref[...] = jnp.zeros_like(carry_ref)

    h1 = h_ref[...] + _mm(mix_ref[...], wout_ref[...])
    h1_ref[...] = h1
    xn = _rmsnorm_rows(h1, nw_ref[...])

    half = D_MODEL // 2
    lo = pltpu.bitcast(xn[:, :half].astype(BF16).astype(F32), jnp.uint32)
    hi = pltpu.bitcast(xn[:, half:].astype(BF16).astype(F32), jnp.uint32)
    xp_ref[...] = (hi & jnp.uint32(0xFFFF0000)) | (lo >> 16)

    xs = _split(xn, 3)
    w0, w1, w2 = wr_ref[0], wr_ref[1], wr_ref[2]
    logits = (_mm(xs[0], w0) + (_mm(xs[0], w1) + _mm(xs[1], w0))
              + (_mm(xs[0], w2) + _mm(xs[1], w1) + _mm(xs[2], w0))) + br_ref[...]

    lane = lax.broadcasted_iota(I32, (tm, LANES), 1)
    lanef = lane.astype(F32)
    neg = jnp.float32(-jnp.inf)
    big = jnp.float32(LANES)

    def first_lane_where(cond):
        return jnp.min(jnp.where(cond, lanef, big), axis=-1, keepdims=True)

    gl = jnp.where(lane < N_GROUPS, logits, neg)
    gmax = jnp.max(gl, axis=-1, keepdims=True)
    gsel = first_lane_where(gl == gmax)
    gprob = 1.0 / jnp.sum(jnp.exp(gl - gmax), axis=-1, keepdims=True)
    lo_lane = ROUTE_LANE0 + EXP_PER_GROUP * gsel
    emask = (lanef >= lo_lane) & (lanef < lo_lane + EXP_PER_GROUP)
    el = jnp.where(emask, logits, neg)
    emax = jnp.max(el, axis=-1, keepdims=True)
    pe = jnp.exp(el - emax)
    prob = pe / jnp.sum(pe, axis=-1, keepdims=True)
    prob = jnp.where(emask, prob, -1.0)
    v1 = jnp.max(prob, axis=-1, keepdims=True)
    i1 = first_lane_where(prob == v1)
    prob2 = jnp.where(lanef == i1, -1.0, prob)
    v2 = jnp.max(prob2, axis=-1, keepdims=True)
    i2 = first_lane_where(prob2 == v2)
    wsum = v1 + v2
    w1c = gprob * v1 / wsum
    w2c = gprob * v2 / wsum

    oh1 = (lanef == i1).astype(F32)
    oh2 = (lanef == i2).astype(F32)
    oh = oh1 + oh2
    before = _mm(ltri_ref[...], oh.astype(BF16)) + carry_ref[...]
    r1 = jnp.sum(before * oh1, axis=-1, keepdims=True)
    r2 = jnp.sum(before * oh2, axis=-1, keepdims=True)
    carry_ref[...] = carry_ref[...] + jnp.sum(oh, axis=0, keepdims=True)
    cnt_ref[...] = carry_ref[...]

    e1 = i1 - ROUTE_LANE0
    e2 = i2 - ROUTE_LANE0
    out = jnp.where(lane == 0, e1, jnp.where(lane == 1, e2, jnp.where(lane == 2, w1c, jnp.where(
        lane == 3, w2c, jnp.where(lane == 4, r1, jnp.where(lane == 5, r2, 0.0))))))
    route_ref[...] = out[:, :8]


def _out_route(h, mixed, wout_bf, norm_w, wr3, br, tm):
    T = h.shape[0]
    ltri = jnp.asarray(np.tril(np.ones((tm, tm)), -1), BF16)
    return pl.pallas_call(
        functools.partial(_out_route_body, tm),
        grid=(T // tm,),
        in_specs=[
            pl.BlockSpec((tm, D_MODEL), lambda i: (i, 0)),
            pl.BlockSpec((tm, D_MODEL), lambda i: (i, 0)),
            pl.BlockSpec((D_MODEL, D_MODEL), lambda i: (0, 0)),
            pl.BlockSpec((1, D_MODEL), lambda i: (0, 0)),
            pl.BlockSpec((3, D_MODEL, LANES), lambda i: (0, 0, 0)),
            pl.BlockSpec((1, LANES), lambda i: (0, 0)),
            pl.BlockSpec((tm, tm), lambda i: (0, 0)),
        ],
        out_specs=(
            pl.BlockSpec((tm, D_MODEL), lambda i: (i, 0)),
            pl.BlockSpec((tm, D_MODEL // 2), lambda i: (i, 0)),
            pl.BlockSpec((tm, 8), lambda i: (i, 0)),
            pl.BlockSpec((1, LANES), lambda i: (0, 0)),
        ),
        out_shape=(
            jax.ShapeDtypeStruct((T, D_MODEL), F32),
            jax.ShapeDtypeStruct((T, D_MODEL // 2), jnp.uint32),
            jax.ShapeDtypeStruct((T, 8), F32),
            jax.ShapeDtypeStruct((1, LANES), F32),
        ),
        scratch_shapes=[pltpu.VMEM((1, LANES), F32)],
        compiler_params=pltpu.CompilerParams(
            dimension_semantics=("arbitrary",), vmem_limit_bytes=VMEM_LIMIT),
        name="out_route",
    )(h, mixed, wout_bf, norm_w.reshape(1, D_MODEL), wr3, br, ltri)


def _dispatch_body(tm, pos_ref, xp_ref, xs_hbm, sem):
    base = pl.program_id(0) * (2 * tm)

    def row_copy(r, k):
        dst = pos_ref[base + 2 * r + k]
        return pltpu.make_async_copy(xp_ref.at[pl.ds(r, 1), :], xs_hbm.at[pl.ds(dst, 1), :], sem)

    def issue(r, c):
        row_copy(r, 0).start()
        row_copy(r, 1).start()
        return c

    def drain(r, c):
        row_copy(r, 0).wait()
        row_copy(r, 1).wait()
        return c

    lax.fori_loop(0, tm, issue, 0)
    lax.fori_loop(0, tm, drain, 0)


def _dispatch(pos, xp, tm):
    T = xp.shape[0]
    return pl.pallas_call(
        functools.partial(_dispatch_body, tm),
        grid_spec=pltpu.PrefetchScalarGridSpec(
            num_scalar_prefetch=1,
            grid=(T // tm,),
            in_specs=[pl.BlockSpec((tm, D_MODEL // 2), lambda i, pos: (i, 0))],
            out_specs=pl.BlockSpec(memory_space=pl.ANY),
            scratch_shapes=[pltpu.SemaphoreType.DMA],
        ),
        out_shape=jax.ShapeDtypeStruct((2 * T, D_MODEL // 2), jnp.uint32),
        compiler_params=pltpu.CompilerParams(
            dimension_semantics=("arbitrary",)),
        name="dispatch",
    )(pos, xp)


def _experts_body(tg, meta_ref, xs_ref, wg_ref, wu_ref, wd_ref, y_ref):
    v = pl.program_id(0)
    first = meta_ref[2, v]
    r0 = meta_ref[3, v]
    r1 = meta_ref[4, v]

    @pl.when(first == 1)
    def _():
        y_ref[...] = jnp.zeros_like(y_ref)

    @pl.when(r1 > r0)
    def _():
        half = D_MODEL // 2
        xw = xs_ref[...]
        x_lo = pltpu.bitcast(xw << 16, F32).astype(BF16)
        x_hi = pltpu.bitcast(xw & jnp.uint32(0xFFFF0000), F32).astype(BF16)
        wg = wg_ref[0, 0]
        wu = wu_ref[0, 0]
        gate = _mm(x_lo, wg[:half].astype(BF16)) + _mm(x_hi, wg[half:].astype(BF16))
        up = _mm(x_lo, wu[:half].astype(BF16)) + _mm(x_hi, wu[half:].astype(BF16))
        hid = (_silu(gate) * up).astype(BF16)
        y = _mm(hid, wd_ref[0, 0].astype(BF16))
        row = lax.broadcasted_iota(I32, (tg, 1), 0)
        y_ref[...] += jnp.where((row >= r0) & (row < r1), y, 0.0)


def _experts(meta, xs, w_gate, w_up, w_down, layer, tg):
    P = xs.shape[0]
    nv = meta.shape[1]
    return pl.pallas_call(
        functools.partial(_experts_body, tg),
        grid_spec=pltpu.PrefetchScalarGridSpec(
            num_scalar_prefetch=1,
            grid=(nv,),
            in_specs=[
                pl.BlockSpec((tg, D_MODEL // 2), lambda v, m: (m[0, v], 0)),
                pl.BlockSpec((1, 1, D_MODEL, D_EXPERT), lambda v, m: (layer, m[1, v], 0, 0)),
                pl.BlockSpec((1, 1, D_MODEL, D_EXPERT), lambda v, m: (layer, m[1, v], 0, 0)),
                pl.BlockSpec((1, 1, D_EXPERT, D_MODEL), lambda v, m: (layer, m[1, v], 0, 0)),
            ],
            out_specs=pl.BlockSpec((tg, D_MODEL), lambda v, m: (m[0, v], 0)),
        ),
        out_shape=jax.ShapeDtypeStruct((P, D_MODEL), F32),
        compiler_params=pltpu.CompilerParams(
            dimension_semantics=("arbitrary",), vmem_limit_bytes=VMEM_LIMIT),
        name="experts",
    )(meta, xs, w_gate, w_up, w_down)


def _visit_metadata(counts, n_tiles, tg):
    nv = n_tiles + N_EXPERTS
    ends = jnp.cumsum(counts)
    offs = ends - counts
    first_tile = offs // tg
    last_tile = jnp.maximum(ends - 1, 0) // tg
    ntile = jnp.where(counts > 0, last_tile - first_tile + 1, 0)
    vend = jnp.cumsum(ntile)
    vstart = vend - ntile
    total = vend[-1]
    v = jnp.arange(nv, dtype=I32)
    e = jnp.minimum(jnp.searchsorted(vend, v, side="right").astype(I32), N_EXPERTS - 1)
    valid = v < total
    e_last = jnp.max(jnp.where(counts > 0, jnp.arange(N_EXPERTS, dtype=I32), 0))
    e = jnp.where(valid, e, e_last)
    tile = jnp.where(valid, first_tile[e] + (v - vstart[e]), n_tiles - 1)
    r0 = jnp.clip(offs[e] - tile * tg, 0, tg)
    r1 = jnp.clip(ends[e] - tile * tg, 0, tg)
    r0 = jnp.where(valid, r0, 0)
    r1 = jnp.where(valid, r1, 0)
    prev_tile = jnp.concatenate([jnp.full((1,), -1, I32), tile[:-1]])
    first = (tile != prev_tile).astype(I32)
    return jnp.stack([tile, e, first, r0, r1]).astype(I32)


def _combine_body(tm, final, pos_ref, h1_ref, route_ref, nw_ref, y_hbm, o_ref, ybuf, sem):
    base = pl.program_id(0) * (2 * tm)

    def row_copy(r, k):
        src = pos_ref[base + 2 * r + k]
        return pltpu.make_async_copy(y_hbm.at[pl.ds(src, 1), :], ybuf.at[k, pl.ds(r, 1), :], sem)

    def issue(r, c):
        row_copy(r, 0).start()
        row_copy(r, 1).start()
        return c

    def drain(r, c):
        row_copy(r, 0).wait()
        row_copy(r, 1).wait()
        return c

    lax.fori_loop(0, tm, issue, 0)
    lax.fori_loop(0, tm, drain, 0)
    rt = route_ref[...]
    h2 = h1_ref[...] + rt[:, 2:3] * ybuf[0] + rt[:, 3:4] * ybuf[1]
    if final:
        h2 = _rmsnorm_rows(h2, nw_ref[...])
    o_ref[...] = h2


def _combine(pos, h1, route, norm_w, y, tm, final):
    T = h1.shape[0]
    return pl.pallas_call(
        functools.partial(_combine_body, tm, final),
        grid_spec=pltpu.PrefetchScalarGridSpec(
            num_scalar_prefetch=1,
            grid=(T // tm,),
            in_specs=[
                pl.BlockSpec((tm, D_MODEL), lambda i, pos: (i, 0)),
                pl.BlockSpec((tm, 8), lambda i, pos: (i, 0)),
                pl.BlockSpec((1, D_MODEL), lambda i, pos: (0, 0)),
                pl.BlockSpec(memory_space=pl.ANY),
            ],
            out_specs=pl.BlockSpec((tm, D_MODEL), lambda i, pos: (i, 0)),
            scratch_shapes=[pltpu.VMEM((2, tm, D_MODEL), F32), pltpu.SemaphoreType.DMA],
        ),
        out_shape=jax.ShapeDtypeStruct((T, D_MODEL), F32),
        compiler_params=pltpu.CompilerParams(
            dimension_semantics=("arbitrary",), vmem_limit_bytes=VMEM_LIMIT),
        name="combine",
    )(pos, h1, route, norm_w.reshape(1, D_MODEL), y)


def _token_tile(T):
    for tm in (256, 128, 64, 32, 16, 8):
        if T % tm == 0:
            return tm
    raise ValueError("token count must be a multiple of 8")


def _pair_states(s):
    n, h = s.shape[0], s.shape[1]
    return s.reshape(n, h // 2, 2 * HEAD_DIM, HEAD_DIM)


def _unpair_states(s, n):
    return s.reshape(n, -1, HEAD_DIM, HEAD_DIM)


def _rot_tables(pos):
    half = HEAD_DIM // 2
    inv = ROPE_THETA ** (-jnp.arange(half, dtype=F32) / half)
    ang = pos.astype(F32)[:, None] * inv[None, :]
    cos, sin = jnp.cos(ang), jnp.sin(ang)
    c = jnp.concatenate([cos, cos], axis=-1)
    s = jnp.concatenate([-sin, sin], axis=-1)
    return jnp.concatenate([c, c], axis=-1), jnp.concatenate([s, s], axis=-1)


def kernel(x_prompt, x_sample, state_dn_conv, state_dn, state_gla, state_ret, norm_mix, w_in, dn_conv_w,
           dn_a_log, dn_dt_bias, dn_norm, gla_w2, gla_b2, gla_norm, ret_norm, w_out, norm_ffn, w_group,
           b_group, w_router, b_router, w_gate, w_up, w_down, norm_final):
    B, L, _ = x_prompt.shape
    DB, LS, _ = x_sample.shape
    depth = w_in.shape[0]
    assert L % ROWS == 0 and ROWS % LS == 0 and DB % (ROWS // LS) == 0
    nseq_s = ROWS // LS
    NBS = DB // nseq_s
    TP, TS = B * L, DB * LS
    T = TP + TS
    tm = _token_tile(T)
    tg = tm
    n_tiles = (2 * T) // tg

    h = jnp.concatenate([x_prompt.reshape(TP, D_MODEL), x_sample.reshape(TS, D_MODEL)], axis=0)

    w_in_r = jnp.concatenate(
        [w_in[:, :, 0:2048], w_in[:, :, 2064:3088], w_in[:, :, 3104:4128], w_in[:, :, 2048:2064],
         w_in[:, :, 3088:3104], jnp.zeros((depth, D_MODEL, PROJ_W - 4128), w_in.dtype)], axis=-1).astype(BF16)
    w_out_bf = w_out.astype(BF16)
    pad_lanes = lambda a, lo: jnp.zeros((depth, LANES), F32).at[:, lo:lo + a.shape[1]].set(a)
    alog = pad_lanes(dn_a_log, 0)
    dtb = pad_lanes(dn_dt_bias, 0)
    w2p = jnp.zeros((depth, LANES, GLA_W), F32).at[:, 16:16 + GLA_RANK, :].set(gla_w2).astype(BF16)
    two = lambda a: jnp.concatenate([a, a], axis=-1)
    nrm = jnp.stack([two(dn_norm), two(gla_norm), two(ret_norm)], axis=1)
    w_rt = jnp.zeros((depth, D_MODEL, LANES), F32)
    w_rt = w_rt.at[:, :, 0:N_GROUPS].set(w_group).at[:, :, ROUTE_LANE0:ROUTE_LANE0 + N_EXPERTS].set(w_router)
    r1 = w_rt - w_rt.astype(BF16).astype(F32)
    r2 = r1 - r1.astype(BF16).astype(F32)
    w_rt3 = jnp.stack([w_rt.astype(BF16), r1.astype(BF16), r2.astype(BF16)], axis=1)
    b_rt = jnp.zeros((depth, 1, LANES), F32)
    b_rt = b_rt.at[:, 0, 0:N_GROUPS].set(b_group).at[:, 0, ROUTE_LANE0:ROUTE_LANE0 + N_EXPERTS].set(b_router)

    rc_p, rs_p = _rot_tables(jnp.arange(L))
    rc_s, rs_s = _rot_tables(PAST_LEN + (jnp.arange(ROWS) % LS))

    zeros_p = lambda hh: jnp.zeros((B, 1, hh // 2, 2 * HEAD_DIM, HEAD_DIM), F32)
    outs = dict(pc=[], pd=[], pg=[], pr=[], sc=[], sd=[], sg=[], sr=[])
    y = None
    for l in range(depth):
        proj = _in_proj(h, norm_mix[l], w_in_r[l], tm)
        params = (dn_conv_w[l], alog[l:l + 1], dtb[l:l + 1], w2p[l], gla_b2[l].reshape(1, GLA_W), nrm[l])
        proj3 = proj.reshape(T // ROWS, ROWS, PROJ_W)
        mp, pc, pd, pg, pr = _mixers(
            proj3, 0, B, L, jnp.zeros((B, 1, CONV_W - 1, 3 * DN_W), F32),
            zeros_p(DN_HEADS), zeros_p(GLA_HEADS), zeros_p(RET_HEADS), rc_p, rs_p, params, 1, ROWS, 1)
        grp = lambda s: _pair_states(s).reshape((NBS, nseq_s) + (s.shape[1] // 2, 2 * HEAD_DIM, HEAD_DIM))
        ms, sc, sd, sg, sr = _mixers(
            proj3, TP // ROWS, NBS, ROWS,
            state_dn_conv[l].reshape(NBS, nseq_s, CONV_W - 1, 3 * DN_W),
            grp(state_dn[l]), grp(jnp.swapaxes(state_gla[l], -1, -2)), grp(state_ret[l]),
            rc_s, rs_s, params, nseq_s, LS, 1)
        outs["pc"].append(pc.reshape(B, CONV_W - 1, 3 * DN_W))
        outs["pd"].append(_unpair_states(pd, B))
        outs["pg"].append(jnp.swapaxes(_unpair_states(pg, B), -1, -2))
        outs["pr"].append(_unpair_states(pr, B))
        outs["sc"].append(sc.reshape(DB, CONV_W - 1, 3 * DN_W))
        outs["sd"].append(_unpair_states(sd, DB))
        outs["sg"].append(jnp.swapaxes(_unpair_states(sg, DB), -1, -2))
        outs["sr"].append(_unpair_states(sr, DB))
        mixed = jnp.concatenate([mp.reshape(TP, D_MODEL), ms.reshape(TS, D_MODEL)], axis=0)

        h1, xp, route, cnt = _out_route(h, mixed, w_out_bf[l], norm_ffn[l], w_rt3[l], b_rt[l], tm)
        counts = cnt[0, ROUTE_LANE0:ROUTE_LANE0 + N_EXPERTS].astype(I32)
        offs = jnp.cumsum(counts) - counts
        eid = route[:, 0:2].astype(I32)
        pos = (offs[eid] + route[:, 4:6].astype(I32)).reshape(2 * T)
        xs = _dispatch(pos, xp, tm)
        meta = _visit_metadata(counts, n_tiles, tg)
        ys = _experts(meta, xs, w_gate, w_up, w_down, l, tg)
        final = l == depth - 1
        h = _combine(pos, h1, route, norm_final, ys, tm, final)
    y = h
    st = lambda k: jnp.stack(outs[k])
    return (y[:TP].reshape(B, L, D_MODEL), y[TP:].reshape(DB, LS, D_MODEL),
            st("pc"), st("pd"), st("pg"), st("pr"), st("sc"), st("sd"), st("sg"), st("sr"))
```

```python
import functools
import math

import numpy as np
import jax
import jax.numpy as jnp
from jax import lax
from jax.experimental import pallas as pl
from jax.experimental.pallas import tpu as pltpu

F32 = jnp.float32
BF16 = jnp.bfloat16
I32 = jnp.int32

D_MODEL = 1024
HEAD_DIM = 64
DN_HEADS = 8
GLA_HEADS = 4
RET_HEADS = 4
DN_W = DN_HEADS * HEAD_DIM
GLA_W = GLA_HEADS * HEAD_DIM
RET_W = RET_HEADS * HEAD_DIM
CONV_W = 4
GLA_RANK = 16
GLA_TAU = 16.0
ROPE_THETA = 10000.0
N_GROUPS = 4
EXP_PER_GROUP = 8
N_EXPERTS = N_GROUPS * EXP_PER_GROUP
D_EXPERT = D_MODEL // 2
EPS = 1e-6
PAST_LEN = 16384

LANES = 128
ROWS = 64
PROJ_BIG = 4096
PROJ_W = PROJ_BIG + LANES
C_QKV, C_DNZ = 0, 1536
C_GQ, C_GK, C_GV, C_GR = 2048, 2304, 2560, 2816
C_RQ, C_RK, C_RV, C_RG = 3072, 3328, 3584, 3840
C_SMALL = 4096
ROUTE_LANE0 = N_GROUPS
VMEM_LIMIT = 48 * 1024 * 1024


def _mm(a, b):
    return jnp.dot(a, b, preferred_element_type=F32)


def _mm_nt(a, b):
    return lax.dot_general(a, b, (((1,), (1,)), ((), ())), preferred_element_type=F32)


def _mm_tn(a, b):
    return lax.dot_general(a, b, (((0,), (0,)), ((), ())), preferred_element_type=F32)


def _split(x, n):
    parts = []
    r = x
    for i in range(n):
        p = r.astype(BF16)
        parts.append(p)
        if i + 1 < n:
            r = r - p.astype(F32)
    return parts


def _xmm(x, cst, n=3):
    acc = None
    for p in _split(x, n):
        t = _mm(p, cst)
        acc = t if acc is None else acc + t
    return acc


def _cmm(cst, x, n=3):
    acc = None
    for p in _split(x, n):
        t = _mm(cst, p)
        acc = t if acc is None else acc + t
    return acc


def _xmm_nt(cst, x, n=3):
    acc = None
    for p in _split(x, n):
        t = _mm_nt(cst, p)
        acc = t if acc is None else acc + t
    return acc


def _mm2(a, b):
    ah, al = _split(a, 2)
    bh, bl = _split(b, 2)
    return _mm(ah, bh) + (_mm(ah, bl) + _mm(al, bh))


def _sigmoid(x):
    return 1.0 / (1.0 + jnp.exp(-x))


def _silu(x):
    return x * _sigmoid(x)


def _softplus(x):
    return jnp.maximum(x, 0.0) + jnp.log1p(jnp.exp(-jnp.abs(x)))


def _rmsnorm_rows(x, w):
    ms = jnp.mean(x * x, axis=-1, keepdims=True)
    return x * lax.rsqrt(ms + EPS) * w


def _split_mm(x, w_ref):
    if w_ref.shape[0] == 1:
        return _mm(x.astype(BF16), w_ref[0])
    xh, xl = _split(x, 2)
    return _mm(xh, w_ref[0]) + (_mm(xl, w_ref[0]) + _mm(xh, w_ref[1]))


def _bf16_terms(w, n):
    terms = []
    r = w
    for i in range(n):
        t = r.astype(BF16)
        terms.append(t)
        if i + 1 < n:
            r = r - t.astype(F32)
    return jnp.stack(terms, axis=-3)


def _in_proj_body(h_ref, nw_ref, w_ref, o_ref):
    xn = _rmsnorm_rows(h_ref[...], nw_ref[...])
    o_ref[...] = _split_mm(xn, w_ref)


def _in_proj(h, norm_w, w_terms, tm):
    T = h.shape[0]
    return pl.pallas_call(
        _in_proj_body,
        grid=(T // tm,),
        in_specs=[
            pl.BlockSpec((tm, D_MODEL), lambda i: (i, 0)),
            pl.BlockSpec((1, D_MODEL), lambda i: (0, 0)),
            pl.BlockSpec(w_terms.shape, lambda i: (0, 0, 0), pipeline_mode=pl.Buffered(1)),
        ],
        out_specs=pl.BlockSpec((tm, PROJ_W), lambda i: (i, 0)),
        out_shape=jax.ShapeDtypeStruct((T, PROJ_W), F32),
        compiler_params=pltpu.CompilerParams(
            dimension_semantics=("arbitrary",), vmem_limit_bytes=VMEM_LIMIT),
        name="in_proj",
    )(h, norm_w.reshape(1, D_MODEL), w_terms)


def _mixer_constants(nseq, cseq, n_blk):
    r = np.arange(ROWS)
    seq = r // cseq
    tl = r % cseq
    same = seq[:, None] == seq[None, :]
    incl = same & (tl[None, :] <= tl[:, None])
    strict = same & (tl[None, :] < tl[:, None])
    two = lambda m: np.concatenate([m, m], axis=1)
    eye = np.eye(ROWS)
    sel = np.zeros((ROWS, LANES))
    sel[:, 0] = 1.0
    sel[:, HEAD_DIM] = 1.0
    lane_head = np.arange(LANES) // HEAD_DIM
    bd = (lane_head[:, None] == lane_head[None, :]).astype(np.float32)
    eg = np.zeros((LANES, DN_W))
    eb = np.zeros((LANES, DN_W))
    for hh in range(DN_HEADS):
        eg[hh, hh * HEAD_DIM:(hh + 1) * HEAD_DIM] = 1.0
        eb[DN_HEADS + hh, hh * HEAD_DIM:(hh + 1) * HEAD_DIM] = 1.0
    rb = ROWS // n_blk
    blk = np.stack([np.broadcast_to((r < rb * (i + 1))[:, None], (ROWS, LANES)) for i in range(n_blk)])
    ii = np.concatenate([np.eye(HEAD_DIM), np.eye(HEAD_DIM)], axis=1)
    consts = dict(
        lcum=jnp.asarray(incl, BF16), lall=jnp.asarray(same, BF16),
        mincl=jnp.asarray(two(incl), F32), mstr=jnp.asarray(two(strict), F32),
        i2=jnp.asarray(two(eye), F32), sel=jnp.asarray(sel, BF16),
        bd=jnp.asarray(bd, F32), bsum=jnp.asarray(bd, BF16),
        eg=jnp.asarray(eg, BF16), eb=jnp.asarray(eb, BF16),
        blk=jnp.asarray(blk, F32),
        ii=jnp.asarray(ii, BF16), iit=jnp.asarray(ii.T, BF16),
    )
    lg = jnp.log(1.0 - 2.0 ** (-5.0 - jnp.arange(RET_HEADS, dtype=F32)))
    tlf = jnp.asarray(tl, F32)
    rel = tlf[:, None] - tlf[None, :]
    dmat = jnp.where(jnp.asarray(incl)[None], jnp.exp(jnp.maximum(rel, 0.0)[None] * lg[:, None, None]), 0.0)
    qin = jnp.exp((tlf + 1.0)[None, :] * lg[:, None])
    kout = jnp.exp((cseq - 1.0 - tlf)[None, :] * lg[:, None])
    sdec = jnp.exp(cseq * lg)
    pair = lambda a, b: jnp.concatenate([a, b], axis=-1)
    lanes = lambda v: jnp.broadcast_to(v[:, None], (ROWS, HEAD_DIM))
    consts["dmat"] = jnp.stack([pair(dmat[2 * p], dmat[2 * p + 1]) for p in range(RET_HEADS // 2)])
    consts["qin"] = jnp.stack([pair(lanes(qin[2 * p]), lanes(qin[2 * p + 1])) for p in range(RET_HEADS // 2)])
    consts["kout"] = jnp.stack([pair(lanes(kout[2 * p]), lanes(kout[2 * p + 1])) for p in range(RET_HEADS // 2)])
    consts["sdec"] = jnp.stack([pair(jnp.full((1, HEAD_DIM), sdec[2 * p]), jnp.full((1, HEAD_DIM), sdec[2 * p + 1]))
                                for p in range(RET_HEADS // 2)])
    return consts


_CONST_ORDER = ("lcum", "lall", "mincl", "mstr", "i2", "sel", "bd", "bsum", "eg", "eb", "blk", "ii", "iit",
                "dmat", "qin", "kout", "sdec")


def _mixer_body(nseq, cseq, n_blk, n_sq, G, hp, *refs):
    proj_refs = refs[:G]
    refs = refs[G:]
    (conv_in_ref, dn_in_ref, gla_in_ref, ret_in_ref, rc_ref, rs_ref,
     convw_ref, alog_ref, dtb_ref, w2_ref, b2_ref, nrm_ref,
     lcum_ref, lall_ref, mincl_ref, mstr_ref, i2_ref, sel_ref, bd_ref, bsum_ref, eg_ref, eb_ref, blk_ref,
     ii_ref, iit_ref, dmat_ref, qin_ref, kout_ref, sdec_ref,
     mixed_ref, conv_out_ref, dn_out_ref, gla_out_ref, ret_out_ref,
     ext_ref, st_ref) = refs
    P_DN, P_GLA, P_RET = 0, DN_HEADS // 2, (DN_HEADS + GLA_HEADS) // 2
    ci = pl.program_id(1)
    last = pl.num_programs(1) - 1

    lane = lax.broadcasted_iota(I32, (1, LANES), 1)
    mlo = (lane < HEAD_DIM).astype(F32)
    mhi = 1.0 - mlo
    half_lo = (lane % HEAD_DIM) < (HEAD_DIM // 2)
    rid = lax.broadcasted_iota(I32, (ROWS, LANES), 0)
    bd = bd_ref[...]
    lcum = lcum_ref[...]
    lall = lall_ref[...]
    mincl = mincl_ref[...]
    mstr = mstr_ref[...]
    bsum = bsum_ref[...]

    def stack(x):
        return jnp.concatenate([x * mlo, x * mhi], axis=0)

    def parts(x):
        return tuple(_split(x, 2)) if hp else (x.astype(BF16),)

    def pmm(ap, bp, f=_mm):
        acc = None
        for i, a in enumerate(ap):
            for j, b in enumerate(bp):
                if i + j < 2:
                    t = f(a, b)
                    acc = t if acc is None else acc + t
        return acc

    def seq_rows(x, j):
        if nseq == 1:
            return x
        return jnp.where((rid >= j * cseq) & (rid < (j + 1) * cseq), x, 0.0)

    def seq_row1(x, j):
        return x[j * cseq:j * cseq + 1, :]

    def state_mm(x, g, p, nt=False):
        acc = None
        for j in range(nseq):
            t = pmm(parts(seq_rows(x, j)), parts(st_ref[g, j, p]), _mm_nt if nt else _mm)
            acc = t if acc is None else acc + t
        return acc

    defer = nseq == 1
    mixed_parts = {}
    new_state = {}

    def put_state(g, j, p, val):
        if defer:
            new_state[(g, p)] = val
        else:
            st_ref[g, j, p] = val

    def finish(g, slot, o2, nrow, zcol):
        ms = _xmm(o2 * o2, bsum, 2) * (1.0 / HEAD_DIM)
        yield
        on = o2 * lax.rsqrt(ms + EPS) * nrm_ref[nrow:nrow + 1, :]
        z2 = proj_refs[g][0, :, zcol:zcol + LANES]
        mixed_parts[(g, slot)] = (on * _silu(z2)).astype(mixed_ref.dtype)

    def round_robin(gens):
        gens = list(gens)
        while gens:
            alive = []
            for ge in gens:
                try:
                    next(ge)
                    alive.append(ge)
                except StopIteration:
                    pass
            gens = alive
            yield

    @pl.when(ci == 0)
    def _():
        for g in range(G):
            ext_ref[g, :, 8 - (CONV_W - 1):8, :] = conv_in_ref[g]
            for j in range(nseq):
                for p in range(DN_HEADS // 2):
                    st_ref[g, j, P_DN + p] = _xmm(dn_in_ref[g, j, p], ii_ref[...]) * bd
                for p in range(GLA_HEADS // 2):
                    st_ref[g, j, P_GLA + p] = _xmm(gla_in_ref[g, j, p], ii_ref[...]) * bd
                for p in range(RET_HEADS // 2):
                    st_ref[g, j, P_RET + p] = _xmm(ret_in_ref[g, j, p], ii_ref[...]) * bd

    def dn_pair(g, p, conv, gcol_all, glast_all, bcol_all):
        ls = slice(p * LANES, (p + 1) * LANES)
        q2 = conv[:, ls]
        k2 = conv[:, DN_W + p * LANES:DN_W + (p + 1) * LANES]
        v2 = conv[:, 2 * DN_W + p * LANES:2 * DN_W + (p + 1) * LANES]
        gcol = gcol_all[:, ls]
        glast = glast_all[:, ls]
        bcol = bcol_all[:, ls]
        qss = _xmm(q2 * q2, bsum, 2)
        kss = _xmm(k2 * k2, bsum, 2)
        ystack = jnp.concatenate([gcol * mlo, gcol * mhi], axis=0)
        grow = _xmm_nt(sel_ref[...], ystack)
        yield
        q2 = q2 * lax.rsqrt(qss + EPS) * (HEAD_DIM ** -0.5)
        k2 = k2 * lax.rsqrt(kss + EPS)
        egc = jnp.exp(gcol)
        decay = jnp.exp(jnp.where(mincl > 0.0, gcol - grow, 0.0)) * mincl
        kst = parts(stack(k2))
        kk = pmm(parts(k2), kst, _mm_nt)
        qk = pmm(parts(q2), kst, _mm_nt)
        oi = state_mm(q2 * egc, g, P_DN + p)
        yield
        nk = -(bcol * kk * decay * mstr)
        tt = i2_ref[...] + nk
        nk = _mm2(nk, jnp.concatenate([nk * mlo, nk * mhi], axis=0))
        yield
        for it in range(n_sq):
            nst = jnp.concatenate([nk * mlo, nk * mhi], axis=0)
            tt = tt + _mm2(tt, nst)
            if it + 1 < n_sq:
                nk = _mm2(nk, nst)
            yield
        ttp = parts(tt)
        wv = pmm(ttp, parts(stack(v2 * bcol)))
        wk = pmm(ttp, parts(stack(k2 * (bcol * egc))))
        yield
        u2 = wv - state_mm(wk, g, P_DN + p)
        yield
        o2 = oi + pmm(parts(qk * decay), parts(stack(u2)))
        kdec = k2 * jnp.exp(glast - gcol)
        eglast = jnp.exp(glast)
        u2p = parts(u2)
        for j in range(nseq):
            upd = pmm(parts(seq_rows(kdec, j)), u2p, _mm_tn) * bd
            put_state(g, j, P_DN + p, st_ref[g, j, P_DN + p] * seq_row1(eglast, j) + upd)
        yield
        yield from finish(g, p, o2, 0, C_DNZ + p * LANES)

    def dn_group(g):
        qkv = proj_refs[g][0, :, C_QKV:C_QKV + 3 * DN_W]
        ext_ref[g, :, 8:8 + cseq, :] = qkv.reshape(nseq, cseq, 3 * DN_W)
        conv = None
        for jj in range(CONV_W):
            st = 8 - (CONV_W - 1) + jj
            term = ext_ref[g, :, st:st + cseq, :] * convw_ref[jj:jj + 1, :]
            conv = term if conv is None else conv + term
        ext_ref[g, :, 8 - (CONV_W - 1):8, :] = ext_ref[g, :, 8 + cseq - (CONV_W - 1):8 + cseq, :]
        conv = _silu(conv.reshape(ROWS, 3 * DN_W))
        sm = proj_refs[g][0, :, C_SMALL:C_SMALL + LANES]
        gdec = -jnp.exp(alog_ref[...]) * _softplus(sm + dtb_ref[...])
        beta = _sigmoid(sm)
        gcum = _cmm(lcum, gdec)
        gtot = _cmm(lall, gdec)
        bcol_all = _xmm(beta, eb_ref[...])
        yield
        gcol_all = _xmm(gcum, eg_ref[...])
        glast_all = _xmm(gtot, eg_ref[...])
        yield
        yield from round_robin(dn_pair(g, p, conv, gcol_all, glast_all, bcol_all) for p in range(DN_HEADS // 2))

    def gla_pair(g, p, bcum_all, btot_all):
        rb = ROWS // n_blk
        ls = slice(p * LANES, (p + 1) * LANES)
        q2 = proj_refs[g][0, :, C_GQ + p * LANES:C_GQ + (p + 1) * LANES] * (HEAD_DIM ** -0.5)
        k2 = proj_refs[g][0, :, C_GK + p * LANES:C_GK + (p + 1) * LANES]
        v2 = proj_refs[g][0, :, C_GV + p * LANES:C_GV + (p + 1) * LANES]
        bc = bcum_all[:, ls]
        bt = btot_all[:, ls]
        blocks = []
        for i in range(n_blk):
            rs = slice(i * rb, (i + 1) * rb)
            if i == 0:
                qe = q2[rs] * jnp.exp(bc[rs])
                ke = k2 * jnp.exp(blk_ref[i] * (-bc))
            else:
                m = bc[i * rb - 1:i * rb, :]
                qe = q2[rs] * jnp.exp(bc[rs] - m)
                ke = k2 * jnp.exp(blk_ref[i] * (m - bc))
            blocks.append(pmm(parts(qe), parts(stack(ke)), _mm_nt))
        oi = state_mm(q2 * jnp.exp(bc), g, P_GLA + p, nt=True)
        kdec = parts(k2 * jnp.exp(bt - bc))
        ebt = jnp.exp(bt)
        for j in range(nseq):
            upd = pmm(parts(seq_rows(v2, j)), kdec, _mm_tn) * bd
            put_state(g, j, P_GLA + p, st_ref[g, j, P_GLA + p] * seq_row1(ebt, j) + upd)
        yield
        aa = (blocks[0] if n_blk == 1 else jnp.concatenate(blocks, axis=0)) * mincl
        o2 = oi + pmm(parts(aa), parts(stack(v2)))
        yield
        yield from finish(g, DN_HEADS // 2 + p, o2, 1, C_GR + p * LANES)

    def gla_group(g):
        sm = proj_refs[g][0, :, C_SMALL:C_SMALL + LANES]
        xg = pmm(parts(sm), parts(w2_ref[...])) + b2_ref[...]
        yield
        lg_all = (jnp.minimum(xg, 0.0) - jnp.log1p(jnp.exp(-jnp.abs(xg)))) * (1.0 / GLA_TAU)
        bcum_all = _cmm(lcum, lg_all)
        btot_all = _cmm(lall, lg_all)
        yield
        yield from round_robin(gla_pair(g, p, bcum_all, btot_all) for p in range(GLA_HEADS // 2))

    def ret_pair(g, p):
        rc = rc_ref[...]
        rsn = rs_ref[...]

        def rot(x):
            swapped = jnp.where(half_lo, pltpu.roll(x, LANES - HEAD_DIM // 2, 1), pltpu.roll(x, HEAD_DIM // 2, 1))
            return x * rc + swapped * rsn

        q2 = rot(proj_refs[g][0, :, C_RQ + p * LANES:C_RQ + (p + 1) * LANES])
        k2 = rot(proj_refs[g][0, :, C_RK + p * LANES:C_RK + (p + 1) * LANES]) * (HEAD_DIM ** -0.5)
        v2 = proj_refs[g][0, :, C_RV + p * LANES:C_RV + (p + 1) * LANES]
        att = pmm(parts(q2), parts(stack(k2)), _mm_nt)
        oi = state_mm(q2 * qin_ref[p], g, P_RET + p)
        kdec = k2 * kout_ref[p]
        v2p = parts(v2)
        for j in range(nseq):
            upd = pmm(parts(seq_rows(kdec, j)), v2p, _mm_tn) * bd
            put_state(g, j, P_RET + p, st_ref[g, j, P_RET + p] * sdec_ref[p] + upd)
        yield
        o2 = oi + pmm(parts(att * dmat_ref[p]), parts(stack(v2)))
        yield
        yield from finish(g, (DN_HEADS + GLA_HEADS) // 2 + p, o2, 2, C_RG + p * LANES)

    def ret_group(g):
        yield from round_robin(ret_pair(g, p) for p in range(RET_HEADS // 2))

    for _ in round_robin(f(g) for g in range(G) for f in (dn_group, gla_group, ret_group)):
        pass

    n_pairs = (DN_HEADS + GLA_HEADS + RET_HEADS) // 2
    for g in range(G):
        mixed_ref[g] = jnp.concatenate([mixed_parts[(g, s)] for s in range(n_pairs)], axis=1)
        if defer:
            st_ref[g, 0] = jnp.stack([new_state[(g, p)] for p in range(n_pairs)])

    @pl.when(ci == last)
    def _():
        for g in range(G):
            conv_out_ref[g] = ext_ref[g, :, 8 - (CONV_W - 1):8, :]
            for j in range(nseq):
                for p in range(DN_HEADS // 2):
                    dn_out_ref[g, j, p] = _xmm(st_ref[g, j, P_DN + p], iit_ref[...])
                for p in range(GLA_HEADS // 2):
                    gla_out_ref[g, j, p] = _xmm(st_ref[g, j, P_GLA + p], iit_ref[...])
                for p in range(RET_HEADS // 2):
                    ret_out_ref[g, j, p] = _xmm(st_ref[g, j, P_RET + p], iit_ref[...])


def _mixers(proj3, row0, NB, Lr, conv_in, dn_in, gla_in, ret_in, rot_c, rot_s, params, nseq, cseq, G, hp):
    n_blk = 4 if cseq == ROWS else 1
    n_sq = int(math.log2(cseq)) - 1
    consts = _mixer_constants(nseq, cseq, n_blk)
    conv_w, alog, dtb, w2, b2, nrm = params
    nchunk = Lr // ROWS

    def full(a):
        nd = a.ndim
        return pl.BlockSpec(a.shape, lambda b, c, _nd=nd: (0,) * _nd)

    def per_group(a):
        nd = a.ndim
        return pl.BlockSpec((G,) + a.shape[1:], lambda b, c, _nd=nd: (b,) + (0,) * (_nd - 1))

    small = [conv_w, alog, dtb, w2, b2, nrm] + [consts[k] for k in _CONST_ORDER]
    in_specs = [
        pl.BlockSpec((1, ROWS, PROJ_W), lambda b, c, _g=g: (row0 + (b * G + _g) * nchunk + c, 0, 0))
        for g in range(G)
    ] + [
        per_group(conv_in), per_group(dn_in), per_group(gla_in), per_group(ret_in),
        pl.BlockSpec((ROWS, LANES), lambda b, c: (c, 0)),
        pl.BlockSpec((ROWS, LANES), lambda b, c: (c, 0)),
    ] + [full(a) for a in small]
    out_shape = (
        jax.ShapeDtypeStruct((NB, Lr, D_MODEL), F32 if hp else BF16),
        jax.ShapeDtypeStruct(conv_in.shape, F32),
        jax.ShapeDtypeStruct(dn_in.shape, F32),
        jax.ShapeDtypeStruct(gla_in.shape, F32),
        jax.ShapeDtypeStruct(ret_in.shape, F32),
    )
    out_specs = (
        pl.BlockSpec((G, ROWS, D_MODEL), lambda b, c: (b, c, 0)),
        per_group(conv_in), per_group(dn_in), per_group(gla_in), per_group(ret_in),
    )
    scratch = [
        pltpu.VMEM((G, nseq, 8 + cseq, 3 * DN_W), F32),
        pltpu.VMEM((G, nseq, (DN_HEADS + GLA_HEADS + RET_HEADS) // 2, LANES, LANES), F32),
    ]
    return pl.pallas_call(
        functools.partial(_mixer_body, nseq, cseq, n_blk, n_sq, G, hp),
        grid=(NB // G, nchunk),
        in_specs=in_specs,
        out_specs=out_specs,
        out_shape=out_shape,
        scratch_shapes=scratch,
        compiler_params=pltpu.CompilerParams(
            dimension_semantics=("arbitrary", "arbitrary"), vmem_limit_bytes=VMEM_LIMIT),
        name="mixers_c%d" % cseq,
    )(*([proj3] * G), conv_in, dn_in, gla_in, ret_in, rot_c, rot_s, *small)


def _out_route_body(tm, h_ref, mix_ref, wout_ref, nw_ref, wr_ref, br_ref, ltri_ref,
                    h1_ref, xp_ref, route_ref, cnt_ref, carry_ref):
    i = pl.program_id(0)

    @pl.when(i == 0)
    def _():
        carry_ref[...] = jnp.zeros_like(carry_ref)

    h1 = h_ref[...] + _split_mm(mix_ref[...], wout_ref)
    h1_ref[...] = h1
    xn = _rmsnorm_rows(h1, nw_ref[...])

    xp_ref[...] = xn

    xs = _split(xn, 3)
    w0, w1, w2 = wr_ref[0], wr_ref[1], wr_ref[2]
    logits = (_mm(xs[0], w0) + (_mm(xs[0], w1) + _mm(xs[1], w0))
              + (_mm(xs[0], w2) + _mm(xs[1], w1) + _mm(xs[2], w0))) + br_ref[...]

    lane = lax.broadcasted_iota(I32, (tm, LANES), 1)
    lanef = lane.astype(F32)
    neg = jnp.float32(-jnp.inf)
    big = jnp.float32(LANES)

    def first_lane_where(cond):
        return jnp.min(jnp.where(cond, lanef, big), axis=-1, keepdims=True)

    gl = jnp.where(lane < N_GROUPS, logits, neg)
    gmax = jnp.max(gl, axis=-1, keepdims=True)
    gsel = first_lane_where(gl == gmax)
    gprob = 1.0 / jnp.sum(jnp.exp(gl - gmax), axis=-1, keepdims=True)
    lo_lane = ROUTE_LANE0 + EXP_PER_GROUP * gsel
    emask = (lanef >= lo_lane) & (lanef < lo_lane + EXP_PER_GROUP)
    el = jnp.where(emask, logits, neg)
    emax = jnp.max(el, axis=-1, keepdims=True)
    pe = jnp.exp(el - emax)
    prob = pe / jnp.sum(pe, axis=-1, keepdims=True)
    prob = jnp.where(emask, prob, -1.0)
    v1 = jnp.max(prob, axis=-1, keepdims=True)
    i1 = first_lane_where(prob == v1)
    prob2 = jnp.where(lanef == i1, -1.0, prob)
    v2 = jnp.max(prob2, axis=-1, keepdims=True)
    i2 = first_lane_where(prob2 == v2)
    wsum = v1 + v2
    w1c = gprob * v1 / wsum
    w2c = gprob * v2 / wsum

    oh1 = (lanef == i1).astype(F32)
    oh2 = (lanef == i2).astype(F32)
    oh = oh1 + oh2
    before = _mm(ltri_ref[...], oh.astype(BF16)) + carry_ref[...]
    r1 = jnp.sum(before * oh1, axis=-1, keepdims=True)
    r2 = jnp.sum(before * oh2, axis=-1, keepdims=True)
    carry_ref[...] = carry_ref[...] + jnp.sum(oh, axis=0, keepdims=True)
    cnt_ref[...] = carry_ref[...]

    e1 = i1 - ROUTE_LANE0
    e2 = i2 - ROUTE_LANE0
    out = jnp.where(lane == 0, e1, jnp.where(lane == 1, e2, jnp.where(lane == 2, w1c, jnp.where(
        lane == 3, w2c, jnp.where(lane == 4, r1, jnp.where(lane == 5, r2, 0.0))))))
    route_ref[...] = out[:, :8]


def _out_route(h, mixed, wout_bf, norm_w, wr3, br, tm):
    T = h.shape[0]
    ltri = jnp.asarray(np.tril(np.ones((tm, tm)), -1), BF16)
    return pl.pallas_call(
        functools.partial(_out_route_body, tm),
        grid=(T // tm,),
        in_specs=[
            pl.BlockSpec((tm, D_MODEL), lambda i: (i, 0)),
            pl.BlockSpec((tm, D_MODEL), lambda i: (i, 0)),
            pl.BlockSpec(wout_bf.shape, lambda i: (0, 0, 0)),
            pl.BlockSpec((1, D_MODEL), lambda i: (0, 0)),
            pl.BlockSpec((3, D_MODEL, LANES), lambda i: (0, 0, 0)),
            pl.BlockSpec((1, LANES), lambda i: (0, 0)),
            pl.BlockSpec((tm, tm), lambda i: (0, 0)),
        ],
        out_specs=(
            pl.BlockSpec((tm, D_MODEL), lambda i: (i, 0)),
            pl.BlockSpec((tm, D_MODEL), lambda i: (i, 0)),
            pl.BlockSpec((tm, 8), lambda i: (i, 0)),
            pl.BlockSpec((1, LANES), lambda i: (0, 0)),
        ),
        out_shape=(
            jax.ShapeDtypeStruct((T, D_MODEL), F32),
            jax.ShapeDtypeStruct((T, D_MODEL), F32),
            jax.ShapeDtypeStruct((T, 8), F32),
            jax.ShapeDtypeStruct((1, LANES), F32),
        ),
        scratch_shapes=[pltpu.VMEM((1, LANES), F32)],
        compiler_params=pltpu.CompilerParams(
            dimension_semantics=("arbitrary",), vmem_limit_bytes=VMEM_LIMIT),
        name="out_route",
    )(h, mixed, wout_bf, norm_w.reshape(1, D_MODEL), wr3, br, ltri)


ROW_DMA_UNROLL = 8


def _dispatch_body(tm, pos_ref, xp_ref, xs_hbm, sem):
    base = pl.program_id(0) * (2 * tm)

    def row_copy(r, k):
        dst = pos_ref[base + 2 * r + k]
        return pltpu.make_async_copy(xp_ref.at[pl.ds(r, 1), :], xs_hbm.at[pl.ds(dst, 1), :], sem)

    def issue(r, c):
        row_copy(r, 0).start()
        row_copy(r, 1).start()
        return c

    lax.fori_loop(0, tm, issue, 0, unroll=ROW_DMA_UNROLL)
    for _ in range(2):
        pltpu.make_async_copy(xp_ref, xs_hbm.at[pl.ds(0, tm), :], sem).wait()


def _dispatch(pos, xp, tm):
    T = xp.shape[0]
    return pl.pallas_call(
        functools.partial(_dispatch_body, tm),
        grid_spec=pltpu.PrefetchScalarGridSpec(
            num_scalar_prefetch=1,
            grid=(T // tm,),
            in_specs=[pl.BlockSpec((tm, D_MODEL), lambda i, pos: (i, 0))],
            out_specs=pl.BlockSpec(memory_space=pl.ANY),
            scratch_shapes=[pltpu.SemaphoreType.DMA],
        ),
        out_shape=jax.ShapeDtypeStruct((2 * T, D_MODEL), F32),
        compiler_params=pltpu.CompilerParams(
            dimension_semantics=("arbitrary",)),
        name="dispatch",
    )(pos, xp)


def _experts_body(tg, meta_ref, xs_ref, wg_ref, wu_ref, wd_ref, y_ref):
    v = pl.program_id(0)
    first = meta_ref[2, v]
    r0 = meta_ref[3, v]
    r1 = meta_ref[4, v]

    @pl.when(first == 1)
    def _():
        y_ref[...] = jnp.zeros_like(y_ref)

    @pl.when(r1 > r0)
    def _():
        x = xs_ref[...].astype(BF16)
        gate = _mm(x, wg_ref[0, 0].astype(BF16))
        up = _mm(x, wu_ref[0, 0].astype(BF16))
        hid = (_silu(gate) * up).astype(BF16)
        y = _mm(hid, wd_ref[0, 0].astype(BF16))
        row = lax.broadcasted_iota(I32, (tg, 1), 0)
        y_ref[...] += jnp.where((row >= r0) & (row < r1), y, 0.0)


def _experts(meta, xs, w_gate, w_up, w_down, layer, tg):
    P = xs.shape[0]
    nv = meta.shape[1]
    return pl.pallas_call(
        functools.partial(_experts_body, tg),
        grid_spec=pltpu.PrefetchScalarGridSpec(
            num_scalar_prefetch=1,
            grid=(nv,),
            in_specs=[
                pl.BlockSpec((tg, D_MODEL), lambda v, m: (m[0, v], 0)),
                pl.BlockSpec((1, 1, D_MODEL, D_EXPERT), lambda v, m: (layer, m[1, v], 0, 0)),
                pl.BlockSpec((1, 1, D_MODEL, D_EXPERT), lambda v, m: (layer, m[1, v], 0, 0)),
                pl.BlockSpec((1, 1, D_EXPERT, D_MODEL), lambda v, m: (layer, m[1, v], 0, 0)),
            ],
            out_specs=pl.BlockSpec((tg, D_MODEL), lambda v, m: (m[0, v], 0)),
        ),
        out_shape=jax.ShapeDtypeStruct((P, D_MODEL), F32),
        compiler_params=pltpu.CompilerParams(
            dimension_semantics=("arbitrary",), vmem_limit_bytes=VMEM_LIMIT),
        name="experts",
    )(meta, xs, w_gate, w_up, w_down)


def _visit_metadata(counts, n_tiles, tg):
    nv = n_tiles + N_EXPERTS
    ends = jnp.cumsum(counts)
    offs = ends - counts
    first_tile = offs // tg
    last_tile = jnp.maximum(ends - 1, 0) // tg
    ntile = jnp.where(counts > 0, last_tile - first_tile + 1, 0)
    vend = jnp.cumsum(ntile)
    vstart = vend - ntile
    total = vend[-1]
    v = jnp.arange(nv, dtype=I32)
    e = jnp.minimum(jnp.sum((vend[None, :] <= v[:, None]).astype(I32), axis=1), N_EXPERTS - 1)
    valid = v < total
    e_last = jnp.max(jnp.where(counts > 0, jnp.arange(N_EXPERTS, dtype=I32), 0))
    e = jnp.where(valid, e, e_last)
    tile = jnp.where(valid, first_tile[e] + (v - vstart[e]), n_tiles - 1)
    r0 = jnp.clip(offs[e] - tile * tg, 0, tg)
    r1 = jnp.clip(ends[e] - tile * tg, 0, tg)
    r0 = jnp.where(valid, r0, 0)
    r1 = jnp.where(valid, r1, 0)
    prev_tile = jnp.concatenate([jnp.full((1,), -1, I32), tile[:-1]])
    first = (tile != prev_tile).astype(I32)
    return jnp.stack([tile, e, first, r0, r1]).astype(I32)


def _combine_body(tm, final, pos_ref, h1_ref, route_ref, nw_ref, y_hbm, o_ref, ybuf, sem):
    base = pl.program_id(0) * (2 * tm)

    def row_copy(r, k):
        src = pos_ref[base + 2 * r + k]
        return pltpu.make_async_copy(y_hbm.at[pl.ds(src, 1), :], ybuf.at[k, pl.ds(r, 1), :], sem)

    def issue(r, c):
        row_copy(r, 0).start()
        row_copy(r, 1).start()
        return c

    lax.fori_loop(0, tm, issue, 0, unroll=ROW_DMA_UNROLL)
    for k in range(2):
        pltpu.make_async_copy(y_hbm.at[pl.ds(0, tm), :], ybuf.at[k], sem).wait()
    rt = route_ref[...]
    h2 = h1_ref[...] + rt[:, 2:3] * ybuf[0] + rt[:, 3:4] * ybuf[1]
    if final:
        h2 = _rmsnorm_rows(h2, nw_ref[...])
    o_ref[...] = h2


def _combine(pos, h1, route, norm_w, y, tm, final):
    T = h1.shape[0]
    return pl.pallas_call(
        functools.partial(_combine_body, tm, final),
        grid_spec=pltpu.PrefetchScalarGridSpec(
            num_scalar_prefetch=1,
            grid=(T // tm,),
            in_specs=[
                pl.BlockSpec((tm, D_MODEL), lambda i, pos: (i, 0)),
                pl.BlockSpec((tm, 8), lambda i, pos: (i, 0)),
                pl.BlockSpec((1, D_MODEL), lambda i, pos: (0, 0)),
                pl.BlockSpec(memory_space=pl.ANY),
            ],
            out_specs=pl.BlockSpec((tm, D_MODEL), lambda i, pos: (i, 0)),
            scratch_shapes=[pltpu.VMEM((2, tm, D_MODEL), F32), pltpu.SemaphoreType.DMA],
        ),
        out_shape=jax.ShapeDtypeStruct((T, D_MODEL), F32),
        compiler_params=pltpu.CompilerParams(
            dimension_semantics=("arbitrary",), vmem_limit_bytes=VMEM_LIMIT),
        name="combine",
    )(pos, h1, route, norm_w.reshape(1, D_MODEL), y)


def _token_tile(T):
    for tm in (256, 128, 64, 32, 16, 8):
        if T % tm == 0:
            return tm
    raise ValueError("token count must be a multiple of 8")


def _pair_states(s):
    n, h = s.shape[0], s.shape[1]
    return s.reshape(n, h // 2, 2 * HEAD_DIM, HEAD_DIM)


def _unpair_states(s, n):
    return s.reshape(n, -1, HEAD_DIM, HEAD_DIM)


def _rot_tables(pos):
    half = HEAD_DIM // 2
    inv = ROPE_THETA ** (-jnp.arange(half, dtype=F32) / half)
    ang = pos.astype(F32)[:, None] * inv[None, :]
    cos, sin = jnp.cos(ang), jnp.sin(ang)
    c = jnp.concatenate([cos, cos], axis=-1)
    s = jnp.concatenate([-sin, sin], axis=-1)
    return jnp.concatenate([c, c], axis=-1), jnp.concatenate([s, s], axis=-1)


def kernel(x_prompt, x_sample, state_dn_conv, state_dn, state_gla, state_ret, norm_mix, w_in, dn_conv_w,
           dn_a_log, dn_dt_bias, dn_norm, gla_w2, gla_b2, gla_norm, ret_norm, w_out, norm_ffn, w_group,
           b_group, w_router, b_router, w_gate, w_up, w_down, norm_final):
    B, L, _ = x_prompt.shape
    DB, LS, _ = x_sample.shape
    depth = w_in.shape[0]
    assert L % ROWS == 0 and ROWS % LS == 0 and DB % (ROWS // LS) == 0
    nseq_s = ROWS // LS
    NBS = DB // nseq_s
    TP, TS = B * L, DB * LS
    T = TP + TS
    tm = _token_tile(T)
    tg = tm
    n_tiles = (2 * T) // tg

    h = jnp.concatenate([x_prompt.reshape(TP, D_MODEL), x_sample.reshape(TS, D_MODEL)], axis=0)

    w_in_r = jnp.concatenate(
        [w_in[:, :, 0:2048], w_in[:, :, 2064:3088], w_in[:, :, 3104:4128], w_in[:, :, 2048:2064],
         w_in[:, :, 3088:3104], jnp.zeros((depth, D_MODEL, PROJ_W - 4128), w_in.dtype)], axis=-1)
    pad_lanes = lambda a, lo: jnp.zeros((depth, LANES), F32).at[:, lo:lo + a.shape[1]].set(a)
    alog = pad_lanes(dn_a_log, 0)
    dtb = pad_lanes(dn_dt_bias, 0)
    w2p = jnp.zeros((depth, LANES, GLA_W), F32).at[:, 16:16 + GLA_RANK, :].set(gla_w2)
    two = lambda a: jnp.concatenate([a, a], axis=-1)
    nrm = jnp.stack([two(dn_norm), two(gla_norm), two(ret_norm)], axis=1)
    w_rt = jnp.zeros((depth, D_MODEL, LANES), F32)
    w_rt = w_rt.at[:, :, 0:N_GROUPS].set(w_group).at[:, :, ROUTE_LANE0:ROUTE_LANE0 + N_EXPERTS].set(w_router)
    r1 = w_rt - w_rt.astype(BF16).astype(F32)
    r2 = r1 - r1.astype(BF16).astype(F32)
    w_rt3 = jnp.stack([w_rt.astype(BF16), r1.astype(BF16), r2.astype(BF16)], axis=1)
    b_rt = jnp.zeros((depth, 1, LANES), F32)
    b_rt = b_rt.at[:, 0, 0:N_GROUPS].set(b_group).at[:, 0, ROUTE_LANE0:ROUTE_LANE0 + N_EXPERTS].set(b_router)

    rc_p, rs_p = _rot_tables(jnp.arange(L))
    rc_s, rs_s = _rot_tables(PAST_LEN + (jnp.arange(ROWS) % LS))

    zeros_p = lambda hh: jnp.zeros((B, 1, hh // 2, 2 * HEAD_DIM, HEAD_DIM), F32)
    outs = dict(pc=[], pd=[], pg=[], pr=[], sc=[], sd=[], sg=[], sr=[])
    y = None
    for l in range(depth):
        hp = l < depth - 1
        nterm = 2 if hp else 1
        proj = _in_proj(h, norm_mix[l], _bf16_terms(w_in_r[l], nterm), tm)
        params = (dn_conv_w[l], alog[l:l + 1], dtb[l:l + 1], w2p[l], gla_b2[l].reshape(1, GLA_W), nrm[l])
        proj3 = proj.reshape(T // ROWS, ROWS, PROJ_W)
        mp, pc, pd, pg, pr = _mixers(
            proj3, 0, B, L, jnp.zeros((B, 1, CONV_W - 1, 3 * DN_W), F32),
            zeros_p(DN_HEADS), zeros_p(GLA_HEADS), zeros_p(RET_HEADS), rc_p, rs_p, params, 1, ROWS,
            2 if B % 2 == 0 else 1, hp)
        grp = lambda s: _pair_states(s).reshape((NBS, nseq_s) + (s.shape[1] // 2, 2 * HEAD_DIM, HEAD_DIM))
        ms, sc, sd, sg, sr = _mixers(
            proj3, TP // ROWS, NBS, ROWS,
            state_dn_conv[l].reshape(NBS, nseq_s, CONV_W - 1, 3 * DN_W),
            grp(state_dn[l]), grp(jnp.swapaxes(state_gla[l], -1, -2)), grp(state_ret[l]),
            rc_s, rs_s, params, nseq_s, LS, 1, False)
        outs["pc"].append(pc.reshape(B, CONV_W - 1, 3 * DN_W))
        outs["pd"].append(_unpair_states(pd, B))
        outs["pg"].append(jnp.swapaxes(_unpair_states(pg, B), -1, -2))
        outs["pr"].append(_unpair_states(pr, B))
        outs["sc"].append(sc.reshape(DB, CONV_W - 1, 3 * DN_W))
        outs["sd"].append(_unpair_states(sd, DB))
        outs["sg"].append(jnp.swapaxes(_unpair_states(sg, DB), -1, -2))
        outs["sr"].append(_unpair_states(sr, DB))
        mixed = jnp.concatenate([mp.reshape(TP, D_MODEL), ms.reshape(TS, D_MODEL).astype(mp.dtype)], axis=0)

        h1, xp, route, cnt = _out_route(h, mixed, _bf16_terms(w_out[l], nterm), norm_ffn[l], w_rt3[l], b_rt[l], tm)
        counts = cnt[0, ROUTE_LANE0:ROUTE_LANE0 + N_EXPERTS].astype(I32)
        offs = jnp.cumsum(counts) - counts
        eid = route[:, 0:2].astype(I32)
        offs_tok = jnp.sum(jnp.where(eid[..., None] == jnp.arange(N_EXPERTS, dtype=I32), offs, 0), axis=-1)
        pos = (offs_tok + route[:, 4:6].astype(I32)).reshape(2 * T)
        xs = _dispatch(pos, xp, tm)
        meta = _visit_metadata(counts, n_tiles, tg)
        ys = _experts(meta, xs, w_gate, w_up, w_down, l, tg)
        final = l == depth - 1
        h = _combine(pos, h1, route, norm_final, ys, tm, final)
    y = h
    st = lambda k: jnp.stack(outs[k])
    return (y[:TP].reshape(B, L, D_MODEL), y[TP:].reshape(DB, LS, D_MODEL),
            st("pc"), st("pd"), st("pg"), st("pr"), st("sc"), st("sd"), st("sg"), st("sr"))
```
